```python
import functools
import jax, jax.numpy as jnp
from jax import lax
import numpy as np

D_MODEL = 2048
BATCH = 4
SEQ = 4096
DEPTH = 1
DEC_BATCH = 32
DEC_SEQ = 4
PAST_LEN = 16384
PAGE_SIZE = 128

N_HEADS = 8
HEAD_DIM = 128
ATTN_WIDTH = N_HEADS * HEAD_DIM
N_KV_HEADS = 2
GROUP = N_HEADS // N_KV_HEADS
KV_WIDTH = N_KV_HEADS * HEAD_DIM
IDX_HEADS = 16
IDX_DIM = 64
TOPK_MAX = 256
QBLOCK = 128
ROPE_THETA = 10000.0
CONV_WIDTH = D_MODEL - ATTN_WIDTH
CONV_KERNEL = 31
D_FF = -(-8 * D_MODEL // (3 * 256)) * 256
EPS = 1e-6
SPLIT_SIZES = (ATTN_WIDTH, KV_WIDTH, KV_WIDTH, IDX_HEADS * IDX_DIM, IDX_DIM, IDX_HEADS, 2 * CONV_WIDTH)
IN_WIDTH = sum(SPLIT_SIZES)

kernel_name = 'dsa_conformer_hybrid_step'


def rms_norm(x, g):
    xf = x.astype(jnp.float32)
    y = xf * lax.rsqrt(jnp.mean(xf * xf, axis=-1, keepdims=True) + EPS)
    return (y * g.astype(jnp.float32)).astype(x.dtype)


def layer_norm(x, g, b):
    xf = x.astype(jnp.float32)
    mu = jnp.mean(xf, axis=-1, keepdims=True)
    xc = xf - mu
    y = xc * lax.rsqrt(jnp.mean(xc * xc, axis=-1, keepdims=True) + EPS)
    return (y * g.astype(jnp.float32) + b.astype(jnp.float32)).astype(x.dtype)


def rope(x, pos):
    half = x.shape[-1] // 2
    freqs = ROPE_THETA ** (-jnp.arange(half, dtype=jnp.float32) / half)
    ang = pos.astype(jnp.float32)[:, None] * freqs[None, :]
    cos = jnp.cos(ang)[:, None, :]
    sin = jnp.sin(ang)[:, None, :]
    xf = x.astype(jnp.float32)
    x1, x2 = xf[..., :half], xf[..., half:]
    return jnp.concatenate([x1 * cos - x2 * sin, x2 * cos + x1 * sin], axis=-1).astype(x.dtype)


def project(n, w_in, pos):
    B, T, _ = n.shape
    p = n @ w_in
    splits = np.cumsum(SPLIT_SIZES)[:-1].tolist()
    q, k, v, qi, ki, wi, u = jnp.split(p, splits, axis=-1)
    q = rope(q.reshape(B, T, N_HEADS, HEAD_DIM), pos)
    k = rope(k.reshape(B, T, N_KV_HEADS, HEAD_DIM), pos)
    v = v.reshape(B, T, N_KV_HEADS, HEAD_DIM)
    qi = rope(qi.reshape(B, T, IDX_HEADS, IDX_DIM), pos)
    ki = rope(ki.reshape(B, T, 1, IDX_DIM), pos)[:, :, 0]
    wi = wi * (IDX_HEADS ** -0.5)
    return q, k, v, qi, ki, wi, u


def indexer_scores(qi, wi, ki):
    dots = jnp.einsum('bqhd,bsd->bqhs', qi, ki, preferred_element_type=jnp.float32) * (IDX_DIM ** -0.5)
    return jnp.einsum('bqh,bqhs->bqs', wi.astype(jnp.float32), jax.nn.relu(dots))


def sparse_attend(q, k_sel, v_sel, valid):
    B, Q = q.shape[:2]
    qg = q.reshape(B, Q, N_KV_HEADS, GROUP, HEAD_DIM)
    s = jnp.einsum('bqkgd,bqskd->bqkgs', qg, k_sel, preferred_element_type=jnp.float32) * (HEAD_DIM ** -0.5)
    s = jnp.where(valid[:, :, None, None, :], s, -jnp.inf)
    p = jax.nn.softmax(s, axis=-1).astype(v_sel.dtype)
    o = jnp.einsum('bqkgs,bqskd->bqkgd', p, v_sel)
    return o.reshape(B, Q, ATTN_WIDTH)


_gather_rows = jax.vmap(lambda rows, idx: rows[idx])


def prompt_attention(q, k, v, qi, ki, wi):
    B, T = q.shape[:2]
    topk = min(TOPK_MAX, T // 4)
    key_pos = jnp.arange(T)

    def block(i):
        start = i * QBLOCK
        qb = lax.dynamic_slice_in_dim(q, start, QBLOCK, axis=1)
        qib = lax.dynamic_slice_in_dim(qi, start, QBLOCK, axis=1)
        wib = lax.dynamic_slice_in_dim(wi, start, QBLOCK, axis=1)
        qpos = start + jnp.arange(QBLOCK)
        sc = indexer_scores(qib, wib, ki)
        sc = jnp.where((key_pos[None, :] <= qpos[:, None])[None], sc, -jnp.inf)
        _, idx = lax.top_k(sc, topk)
        valid = idx <= qpos[None, :, None]
        return sparse_attend(qb, _gather_rows(k, idx), _gather_rows(v, idx), valid)

    out = lax.map(block, jnp.arange(T // QBLOCK))
    return out.transpose(1, 0, 2, 3).reshape(B, T, ATTN_WIDTH)


def sample_attention(q, k_new, v_new, qi, ki_new, wi, cache_k, cache_v, cache_idx_k, page_table):
    B, T = q.shape[:2]
    past = page_table.shape[1] * PAGE_SIZE
    L = past + T
    topk = min(TOPK_MAX, L // 4)
    ki_past = cache_idx_k[page_table].reshape(B, past, IDX_DIM)
    ki_all = jnp.concatenate([ki_past, ki_new.astype(ki_past.dtype)], axis=1)
    qpos = past + jnp.arange(T)
    sc = indexer_scores(qi, wi, ki_all)
    sc = jnp.where((jnp.arange(L)[None, :] <= qpos[:, None])[None], sc, -jnp.inf)
    _, idx = lax.top_k(sc, topk)
    valid = idx <= qpos[None, :, None]
    in_past = idx < past
    pidx = jnp.minimum(idx, past - 1)
    phys = page_table[jnp.arange(B)[:, None, None], pidx // PAGE_SIZE]
    slot = pidx % PAGE_SIZE
    nidx = jnp.clip(idx - past, 0, T - 1)
    k_sel = jnp.where(in_past[..., None, None], cache_k[phys, slot], _gather_rows(k_new, nidx))
    v_sel = jnp.where(in_past[..., None, None], cache_v[phys, slot], _gather_rows(v_new, nidx))
    return sparse_attend(q, k_sel, v_sel, valid)


def conformer_conv(u, conv_prev, conv_dw, conv_dw_bias, conv_ln_g, conv_ln_b):
    a, gate = jnp.split(u, 2, axis=-1)
    g = a * jax.nn.sigmoid(gate)
    buf = jnp.concatenate([conv_prev.astype(g.dtype), g], axis=1)
    y = lax.conv_general_dilated(buf, conv_dw[:, None, :].astype(buf.dtype), window_strides=(1,),
                                 padding='VALID', dimension_numbers=('NWC', 'WIO', 'NWC'),
                                 feature_group_count=CONV_WIDTH)
    y = jax.nn.silu(layer_norm(y + conv_dw_bias, conv_ln_g, conv_ln_b))
    return y, buf[:, -(CONV_KERNEL - 1):]


def hybrid_layer(x, pos, conv_prev, attend, norm_attn, w_in, conv_dw, conv_dw_bias, conv_ln_g, conv_ln_b,
                 w_out, norm_ffn, w_gate, w_up, w_down):
    n = rms_norm(x, norm_attn)
    q, k, v, qi, ki, wi, u = project(n, w_in, pos)
    a = attend(q, k, v, qi, ki, wi)
    c, conv_new = conformer_conv(u, conv_prev, conv_dw, conv_dw_bias, conv_ln_g, conv_ln_b)
    h = x + jnp.concatenate([a, c], axis=-1) @ w_out
    n2 = rms_norm(h, norm_ffn)
    h = h + (jax.nn.silu(n2 @ w_gate) * (n2 @ w_up)) @ w_down
    return h, k, v, ki, conv_new


def setup_inputs(seed: int = 0) -> dict:
    key = jax.random.key(seed)
    ks = jax.random.split(key, 20)
    f32 = jnp.float32
    n_pages = PAST_LEN // PAGE_SIZE
    n_pool = (DEC_BATCH * n_pages * 5) // 4

    def nrm(k, shape, s):
        return jax.random.normal(k, shape, f32) * s

    page_table = jax.random.permutation(ks[6], n_pool)[:DEC_BATCH * n_pages].reshape(DEC_BATCH, n_pages).astype(jnp.int32)
    return {
        'x_prompt': nrm(ks[0], (BATCH, SEQ, D_MODEL), 1.0),
        'x_sample': nrm(ks[1], (DEC_BATCH, DEC_SEQ, D_MODEL), 1.0),
        'cache_k': nrm(ks[2], (DEPTH, n_pool, PAGE_SIZE, N_KV_HEADS, HEAD_DIM), 1.0),
        'cache_v': nrm(ks[3], (DEPTH, n_pool, PAGE_SIZE, N_KV_HEADS, HEAD_DIM), 1.0),
        'cache_idx_k': nrm(ks[4], (DEPTH, n_pool, PAGE_SIZE, IDX_DIM), 1.0),
        'state_conv': nrm(ks[5], (DEPTH, DEC_BATCH, CONV_KERNEL - 1, CONV_WIDTH), 0.5),
        'page_table': page_table,
        'norm_attn': 1.0 + nrm(ks[7], (DEPTH, D_MODEL), 0.05),
        'w_in': nrm(ks[8], (DEPTH, D_MODEL, IN_WIDTH), D_MODEL ** -0.5),
        'conv_dw': nrm(ks[9], (DEPTH, CONV_KERNEL, CONV_WIDTH), CONV_KERNEL ** -0.5),
        'conv_dw_bias': nrm(ks[10], (DEPTH, CONV_WIDTH), 0.02),
        'conv_ln_g': 1.0 + nrm(ks[11], (DEPTH, CONV_WIDTH), 0.05),
        'conv_ln_b': nrm(ks[12], (DEPTH, CONV_WIDTH), 0.02),
        'w_out': nrm(ks[13], (DEPTH, D_MODEL, D_MODEL), D_MODEL ** -0.5),
        'norm_ffn': 1.0 + nrm(ks[14], (DEPTH, D_MODEL), 0.05),
        'w_gate': nrm(ks[15], (DEPTH, D_MODEL, D_FF), D_MODEL ** -0.5),
        'w_up': nrm(ks[16], (DEPTH, D_MODEL, D_FF), D_MODEL ** -0.5),
        'w_down': nrm(ks[17], (DEPTH, D_FF, D_MODEL), D_FF ** -0.5),
        'norm_final': 1.0 + nrm(ks[18], (D_MODEL,), 0.05),
    }


def reference(x_prompt, x_sample, cache_k, cache_v, cache_idx_k, state_conv, page_table,
              norm_attn, w_in, conv_dw, conv_dw_bias, conv_ln_g, conv_ln_b, w_out, norm_ffn,
              w_gate, w_up, w_down, norm_final):
    B, T = x_prompt.shape[:2]
    Bs, Ts = x_sample.shape[:2]
    past = page_table.shape[1] * PAGE_SIZE
    pos_p = jnp.arange(T, dtype=jnp.int32)
    pos_s = past + jnp.arange(Ts, dtype=jnp.int32)
    conv_zero = jnp.zeros((B, CONV_KERNEL - 1, CONV_WIDTH), x_prompt.dtype)
    hp, hs = x_prompt, x_sample
    kp, vp, ip, cp, kss, vss, iss, css = [], [], [], [], [], [], [], []
    for l in range(DEPTH):
        weights = (norm_attn[l], w_in[l], conv_dw[l], conv_dw_bias[l], conv_ln_g[l], conv_ln_b[l],
                   w_out[l], norm_ffn[l], w_gate[l], w_up[l], w_down[l])
        hp, k_, v_, i_, c_ = hybrid_layer(hp, pos_p, conv_zero, prompt_attention, *weights)
        kp.append(k_); vp.append(v_); ip.append(i_); cp.append(c_)
        attend_s = functools.partial(sample_attention, cache_k=cache_k[l], cache_v=cache_v[l],
                                     cache_idx_k=cache_idx_k[l], page_table=page_table)
        hs, k_, v_, i_, c_ = hybrid_layer(hs, pos_s, state_conv[l], attend_s, *weights)
        kss.append(k_); vss.append(v_); iss.append(i_); css.append(c_)
    y_prompt = rms_norm(hp, norm_final)
    y_sample = rms_norm(hs, norm_final)
    return (y_prompt, y_sample,
            jnp.stack(kp), jnp.stack(vp), jnp.stack(ip), jnp.stack(cp),
            jnp.stack(kss), jnp.stack(vss), jnp.stack(iss), jnp.stack(css))
```

```python
import functools

import numpy as np
import jax
import jax.numpy as jnp
from jax import lax
from jax.experimental import pallas as pl
from jax.experimental.pallas import tpu as pltpu

N_HEADS = 8
HEAD_DIM = 128
N_KV_HEADS = 2
GROUP = N_HEADS // N_KV_HEADS
IDX_HEADS = 16
IDX_DIM = 64
TOPK_MAX = 256
ROPE_THETA = 10000.0
CONV_KERNEL = 31
EPS = 1e-6
PAGE = 128

LANES = 128
HALO = 32
VMEM_LIMIT = 56 * 1024 * 1024
INT_MIN = np.int32(-2 ** 31)
NEG_BIG = -1e30

BF16 = jnp.bfloat16
F32 = jnp.float32
I32 = jnp.int32


def _params(*sem):
    return pltpu.CompilerParams(dimension_semantics=sem, vmem_limit_bytes=VMEM_LIMIT)


def _order_key(x):
    b = pltpu.bitcast(x, I32)
    return b ^ ((b >> 31) & np.int32(0x7FFFFFFF))


def _rms_mm_kernel(x_ref, g_ref, *refs, n_w, combine):
    w_refs, o_ref, n_ref = refs[:n_w], refs[n_w], refs[n_w + 1]

    @pl.when(pl.program_id(1) == 0)
    def _():
        x = x_ref[...]
        y = x * lax.rsqrt(jnp.mean(x * x, axis=-1, keepdims=True) + EPS)
        n_ref[...] = (y * g_ref[...]).astype(BF16)

    n = n_ref[...]
    outs = [jnp.dot(n, w[...], preferred_element_type=F32) for w in w_refs]
    o_ref[...] = combine(*outs).astype(o_ref.dtype)


def rms_mm(x, g, ws, combine, out_dtype, tm, tn):
    n_tok, d = x.shape
    f = ws[0].shape[1]
    kern = functools.partial(_rms_mm_kernel, n_w=len(ws), combine=combine)
    return pl.pallas_call(
        kern,
        grid=(n_tok // tm, f // tn),
        in_specs=[pl.BlockSpec((tm, d), lambda i, j: (i, 0)),
                  pl.BlockSpec((1, d), lambda i, j: (0, 0))]
                 + [pl.BlockSpec((d, tn), lambda i, j: (0, j)) for _ in ws],
        out_specs=pl.BlockSpec((tm, tn), lambda i, j: (i, j)),
        out_shape=jax.ShapeDtypeStruct((n_tok, f), out_dtype),
        scratch_shapes=[pltpu.VMEM((tm, d), BF16)],
        compiler_params=_params("parallel", "arbitrary"),
        name="rms_mm",
    )(x, g, *ws)


def _dual_mm_kernel(n_ref, w1_ref, w2_ref, o_ref):
    n = n_ref[...]
    a = jnp.dot(n, w1_ref[...], preferred_element_type=F32)
    b = jnp.dot(n, w2_ref[...], preferred_element_type=F32)
    o_ref[...] = (a * jax.nn.sigmoid(a) * b).astype(o_ref.dtype)


def ffn_up(n, w_gate, w_up, tm, tn):
    n_tok, d = n.shape
    f = w_gate.shape[1]
    return pl.pallas_call(
        _dual_mm_kernel,
        grid=(n_tok // tm, f // tn),
        in_specs=[pl.BlockSpec((tm, d), lambda i, j: (i, 0)),
                  pl.BlockSpec((d, tn), lambda i, j: (0, j)),
                  pl.BlockSpec((d, tn), lambda i, j: (0, j))],
        out_specs=pl.BlockSpec((tm, tn), lambda i, j: (i, j)),
        out_shape=jax.ShapeDtypeStruct((n_tok, f), BF16),
        compiler_params=_params("parallel", "arbitrary"),
        name="ffn_up",
    )(n, w_gate, w_up)


def _out_proj_kernel(a_ref, c_ref, wa_ref, wc_ref, x_ref, g_ref, h_ref, n2_ref):
    h = x_ref[...]
    h = h + jnp.dot(a_ref[...], wa_ref[...], preferred_element_type=F32)
    h = h + jnp.dot(c_ref[...], wc_ref[...], preferred_element_type=F32)
    h_ref[...] = h
    y = h * lax.rsqrt(jnp.mean(h * h, axis=-1, keepdims=True) + EPS)
    n2_ref[...] = (y * g_ref[...]).astype(BF16)


def out_proj(a, c, wa, wc, x, g, tm):
    n_tok, d = x.shape
    ka, kc = a.shape[1], c.shape[1]
    return pl.pallas_call(
        _out_proj_kernel,
        grid=(n_tok // tm,),
        in_specs=[pl.BlockSpec((tm, ka), lambda i: (i, 0)),
                  pl.BlockSpec((tm, kc), lambda i: (i, 0)),
                  pl.BlockSpec((ka, d), lambda i: (0, 0)),
                  pl.BlockSpec((kc, d), lambda i: (0, 0)),
                  pl.BlockSpec((tm, d), lambda i: (i, 0)),
                  pl.BlockSpec((1, d), lambda i: (0, 0))],
        out_specs=[pl.BlockSpec((tm, d), lambda i: (i, 0)),
                   pl.BlockSpec((tm, d), lambda i: (i, 0))],
        out_shape=[jax.ShapeDtypeStruct((n_tok, d), F32),
                   jax.ShapeDtypeStruct((n_tok, d), BF16)],
        compiler_params=_params("parallel"),
        name="out_proj",
    )(a, c, wa, wc, x, g)


def _down_kernel(a_ref, w_ref, h_ref, g_ref, y_ref, acc_ref):
    k = pl.program_id(1)

    @pl.when(k == 0)
    def _():
        acc_ref[...] = h_ref[...]

    acc_ref[...] += jnp.dot(a_ref[...], w_ref[...], preferred_element_type=F32)

    @pl.when(k == pl.num_programs(1) - 1)
    def _():
        h = acc_ref[...]
        y = h * lax.rsqrt(jnp.mean(h * h, axis=-1, keepdims=True) + EPS)
        y_ref[...] = y * g_ref[...]


def ffn_down(act, w_down, h, g, tm, tk):
    n_tok, f = act.shape
    d = w_down.shape[1]
    return pl.pallas_call(
        _down_kernel,
        grid=(n_tok // tm, f // tk),
        in_specs=[pl.BlockSpec((tm, tk), lambda i, k: (i, k)),
                  pl.BlockSpec((tk, d), lambda i, k: (k, 0)),
                  pl.BlockSpec((tm, d), lambda i, k: (i, 0)),
                  pl.BlockSpec((1, d), lambda i, k: (0, 0))],
        out_specs=pl.BlockSpec((tm, d), lambda i, k: (i, 0)),
        out_shape=jax.ShapeDtypeStruct((n_tok, d), F32),
        scratch_shapes=[pltpu.VMEM((tm, d), F32)],
        compiler_params=_params("parallel", "arbitrary"),
        name="ffn_down",
    )(act, w_down, h, g)


def _rope128(x, cos, sin):
    return x * cos + pltpu.roll(x, 64, 1) * sin


def _rope64(x, cos, sin, lane):
    partner = jnp.where((lane & 63) < 32, pltpu.roll(x, 96, 1), pltpu.roll(x, 32, 1))
    return x * cos + partner * sin


def _attn_prep_kernel(p_ref, c128_ref, s128_ref, c64_ref, s64_ref,
                      q_ref, kf_ref, kb_ref, vf_ref, vb_ref, qi_ref, kif_ref, kib_ref, wi_ref):
    c128, s128 = c128_ref[...], s128_ref[...]
    c64, s64 = c64_ref[...], s64_ref[...]
    tm = p_ref.shape[0]
    lane = lax.broadcasted_iota(I32, (tm, LANES), 1)
    for h in range(N_HEADS):
        x = p_ref[:, h * 128:(h + 1) * 128]
        q_ref[:, h * 128:(h + 1) * 128] = _rope128(x, c128, s128).astype(BF16)
    for h in range(N_KV_HEADS):
        x = p_ref[:, 1024 + h * 128:1024 + (h + 1) * 128]
        r = _rope128(x, c128, s128)
        kf_ref[:, h * 128:(h + 1) * 128] = r
        kb_ref[:, h * 128:(h + 1) * 128] = r.astype(BF16)
    v = p_ref[:, 1280:1536]
    vf_ref[...] = v
    vb_ref[...] = v.astype(BF16)
    for hp in range(IDX_HEADS // 2):
        x = p_ref[:, 1536 + hp * 128:1536 + (hp + 1) * 128]
        r = _rope64(x, c64, s64, lane)
        qi_ref[2 * hp] = r[:, :64].astype(BF16)
        qi_ref[2 * hp + 1] = r[:, 64:].astype(BF16)
    x = p_ref[:, 2560:2688]
    r = _rope64(x, c64, s64, lane)
    kif_ref[...] = r[:, :64]
    kib_ref[...] = r[:, :64].astype(BF16)
    wi_ref[...] = x[:, 64:64 + IDX_HEADS] * (IDX_HEADS ** -0.5 * IDX_DIM ** -0.5)


def attn_prep(p, tabs, n_pos_tiles, tm):
    n_tok = p.shape[0]
    tab_spec = pl.BlockSpec((tm, LANES), lambda i: (i % n_pos_tiles, 0))
    row = lambda w: pl.BlockSpec((tm, w), lambda i: (i, 0))
    sds = jax.ShapeDtypeStruct
    return pl.pallas_call(
        _attn_prep_kernel,
        grid=(n_tok // tm,),
        in_specs=[pl.BlockSpec((tm, p.shape[1]), lambda i: (i, 0))] + [tab_spec] * 4,
        out_specs=[row(1024), row(256), row(256), row(256), row(256),
                   pl.BlockSpec((IDX_HEADS, tm, IDX_DIM), lambda i: (0, i, 0)),
                   row(IDX_DIM), row(IDX_DIM), row(IDX_HEADS)],
        out_shape=[sds((n_tok, 1024), BF16), sds((n_tok, 256), F32), sds((n_tok, 256), BF16),
                   sds((n_tok, 256), F32), sds((n_tok, 256), BF16),
                   sds((IDX_HEADS, n_tok, IDX_DIM), BF16),
                   sds((n_tok, IDX_DIM), F32), sds((n_tok, IDX_DIM), BF16),
                   sds((n_tok, IDX_HEADS), F32)],
        compiler_params=_params("parallel"),
        name="attn_prep",
    )(p, *tabs)


def rope_tables(pos):
    pos = pos.astype(F32)[:, None]

    def tab(half):
        freqs = ROPE_THETA ** (-jnp.arange(half, dtype=F32) / half)
        ang = pos * freqs[None, :]
        c, s = jnp.cos(ang), jnp.sin(ang)
        reps = LANES // (2 * half)
        return jnp.tile(jnp.concatenate([c, c], -1), (1, reps)), jnp.tile(jnp.concatenate([-s, s], -1), (1, reps))

    c128, s128 = tab(HEAD_DIM // 2)
    c64, s64 = tab(IDX_DIM // 2)
    return c128, s128, c64, s64


def _kth_largest_key(count_ge, k):
    def body(it, acc):
        bit = lax.shift_left(np.int32(1), np.int32(31) - it)
        cand = acc | bit
        return jnp.where(count_ge(cand ^ INT_MIN) >= k, cand, acc)
    acc = lax.fori_loop(0, 32, body, jnp.zeros_like(k))
    return acc ^ INT_MIN


def _prompt_attn_kernel(q_ref, k_ref, v_ref, qi_ref, ki_ref, wit_ref, o_ref,
                        sc_ref, m_ref, l_ref, acc_ref, *, topk):
    i = pl.program_id(1)
    nkb = i + 1
    tq = q_ref.shape[1]
    scale = HEAD_DIM ** -0.5

    qi_all = qi_ref[...].reshape(IDX_HEADS * tq, IDX_DIM)
    qpos = i * tq + lax.broadcasted_iota(I32, (tq, tq), 1)

    def score_block(j, carry):
        kb = ki_ref[0, pl.ds(pl.multiple_of(j * tq, tq), tq), :]
        d = lax.dot_general(kb, qi_all, (((1,), (1,)), ((), ())), preferred_element_type=F32)
        tot = jnp.zeros((tq, tq), F32)
        for h in range(IDX_HEADS):
            tot = tot + wit_ref[0, h:h + 1, :] * jnp.maximum(d[:, h * tq:(h + 1) * tq], 0.0)
        kpos = j * tq + lax.broadcasted_iota(I32, (tq, tq), 0)
        tot = jnp.where(kpos <= qpos, tot, -jnp.inf)
        sc_ref[pl.ds(pl.multiple_of(j * tq, tq), tq), :] = _order_key(tot)
        return carry

    lax.fori_loop(0, nkb, score_block, 0)

    kq = jnp.minimum(topk, i * tq + 1 + lax.broadcasted_iota(I32, (1, tq), 1))

    def count_ge(cand):
        def blk(j, acc):
            s = sc_ref[pl.ds(pl.multiple_of(j * tq, tq), tq), :]
            hit = jnp.where(s >= cand, 1, 0).astype(I32)
            return acc + jnp.sum(hit.reshape(tq // 8, 8, tq), axis=0)
        acc = lax.fori_loop(0, nkb, blk, jnp.zeros((8, tq), I32))
        return jnp.sum(acc, axis=0, keepdims=True)

    thr = _kth_largest_key(count_ge, kq)

    m_ref[...] = jnp.full(m_ref.shape, NEG_BIG, F32)
    l_ref[...] = jnp.zeros(l_ref.shape, F32)
    acc_ref[...] = jnp.zeros(acc_ref.shape, F32)

    def attn_block(j, carry):
        rows = pl.ds(pl.multiple_of(j * tq, tq), tq)
        sel_t = jnp.where(sc_ref[rows, :] >= thr, 1.0, 0.0).astype(F32)
        sel = jnp.transpose(sel_t) > 0.5
        for h in range(N_HEADS):
            g = h // GROUP
            qh = q_ref[0, :, h * HEAD_DIM:(h + 1) * HEAD_DIM]
            kh = k_ref[0, rows, g * HEAD_DIM:(g + 1) * HEAD_DIM]
            vh = v_ref[0, rows, g * HEAD_DIM:(g + 1) * HEAD_DIM]
            s = lax.dot_general(qh, kh, (((1,), (1,)), ((), ())), preferred_element_type=F32) * scale
            s = jnp.where(sel, s, NEG_BIG)
            m_old = m_ref[h]
            m_new = jnp.maximum(m_old, jnp.max(s, axis=1, keepdims=True))
            p = jnp.where(sel, jnp.exp(s - m_new), 0.0)
            alpha = jnp.exp(m_old - m_new)
            l_ref[h] = alpha * l_ref[h] + jnp.sum(p, axis=1, keepdims=True)
            acc_ref[h] = alpha * acc_ref[h] + jnp.dot(p.astype(BF16), vh, preferred_element_type=F32)
            m_ref[h] = m_new
        return carry

    lax.fori_loop(0, nkb, attn_block, 0)

    for h in range(N_HEADS):
        o_ref[0, :, h * HEAD_DIM:(h + 1) * HEAD_DIM] = (acc_ref[h] / l_ref[h]).astype(o_ref.dtype)


def prompt_attention(q, k, v, qi, ki, wit, batch, seq, tq=LANES):
    nt = seq // tq
    q3 = q.reshape(batch, seq, N_HEADS * HEAD_DIM)
    k3 = k.reshape(batch, seq, N_KV_HEADS * HEAD_DIM)
    v3 = v.reshape(batch, seq, N_KV_HEADS * HEAD_DIM)
    ki3 = ki.reshape(batch, seq, IDX_DIM)
    out = pl.pallas_call(
        functools.partial(_prompt_attn_kernel, topk=min(TOPK_MAX, seq // 4)),
        grid=(batch, nt),
        in_specs=[pl.BlockSpec((1, tq, N_HEADS * HEAD_DIM), lambda b, i: (b, i, 0)),
                  pl.BlockSpec((1, seq, N_KV_HEADS * HEAD_DIM), lambda b, i: (b, 0, 0)),
                  pl.BlockSpec((1, seq, N_KV_HEADS * HEAD_DIM), lambda b, i: (b, 0, 0)),
                  pl.BlockSpec((IDX_HEADS, tq, IDX_DIM), lambda b, i: (0, b * nt + i, 0)),
                  pl.BlockSpec((1, seq, IDX_DIM), lambda b, i: (b, 0, 0)),
                  pl.BlockSpec((1, IDX_HEADS, tq), lambda b, i: (b, 0, i))],
        out_specs=pl.BlockSpec((1, tq, N_HEADS * HEAD_DIM), lambda b, i: (b, i, 0)),
        out_shape=jax.ShapeDtypeStruct((batch, seq, N_HEADS * HEAD_DIM), BF16),
        scratch_shapes=[pltpu.VMEM((seq, tq), I32),
                        pltpu.VMEM((N_HEADS, tq, 1), F32),
                        pltpu.VMEM((N_HEADS, tq, 1), F32),
                        pltpu.VMEM((N_HEADS, tq, HEAD_DIM), F32)],
        compiler_params=_params("parallel", "arbitrary"),
        name="prompt_attn",
    )(q3, k3, v3, qi, ki3, wit)
    return out.reshape(batch * seq, N_HEADS * HEAD_DIM)


def _conv_kernel(g_ref, gh_ref, prev_ref, dw_ref, b_ref, lg_ref, lb_ref, o_ref, buf_ref):
    i = pl.program_id(1)
    tt = g_ref.shape[1]
    buf_ref[HALO:, :] = g_ref[0]

    @pl.when(i == 0)
    def _():
        buf_ref[:HALO, :] = prev_ref[0]

    @pl.when(i > 0)
    def _():
        buf_ref[:HALO, :] = gh_ref[0]

    off = HALO - (CONV_KERNEL - 1)
    y = jnp.zeros((tt, g_ref.shape[2]), F32)
    for j in range(CONV_KERNEL):
        y = y + buf_ref[off + j:off + j + tt, :] * dw_ref[j:j + 1, :]
    y = y + b_ref[...]
    mu = jnp.mean(y, axis=-1, keepdims=True)
    yc = y - mu
    z = yc * lax.rsqrt(jnp.mean(yc * yc, axis=-1, keepdims=True) + EPS)
    z = z * lg_ref[...] + lb_ref[...]
    o_ref[0] = (z * jax.nn.sigmoid(z)).astype(o_ref.dtype)


def conformer_conv(g, prev, dw, bias, ln_g, ln_b, tt):
    batch, seq, ch = g.shape
    per = tt // HALO
    halo_src = g if seq > tt else prev
    return pl.pallas_call(
        _conv_kernel,
        grid=(batch, seq // tt),
        in_specs=[pl.BlockSpec((1, tt, ch), lambda b, i: (b, i, 0)),
                  pl.BlockSpec((1, HALO, ch), lambda b, i: (b, jnp.maximum(i * per - 1, 0), 0)),
                  pl.BlockSpec((1, HALO, ch), lambda b, i: (b, 0, 0)),
                  pl.BlockSpec((HALO, ch), lambda b, i: (0, 0)),
                  pl.BlockSpec((1, ch), lambda b, i: (0, 0)),
                  pl.BlockSpec((1, ch), lambda b, i: (0, 0)),
                  pl.BlockSpec((1, ch), lambda b, i: (0, 0))],
        out_specs=pl.BlockSpec((1, tt, ch), lambda b, i: (b, i, 0)),
        out_shape=jax.ShapeDtypeStruct((batch, seq, ch), BF16),
        scratch_shapes=[pltpu.VMEM((HALO + tt, ch), F32)],
        compiler_params=_params("parallel", "arbitrary"),
        name="conformer_conv",
    )(g, halo_src, prev, dw, bias, ln_g, ln_b)


PPS = 16
PPA = 8
QPAD = 8


def _sample_scores_kernel(pt_ref, qi_ref, w_ref, kin_ref, *refs):
    pages, (i_ref, inew_ref) = refs[:PPS], refs[PPS:]
    s = pl.program_id(1)
    qi = qi_ref[0]
    w = w_ref[0]

    def scores(keys_bf16):
        d = lax.dot_general(qi, keys_bf16, (((1,), (1,)), ((), ())), preferred_element_type=F32)
        e = w * jnp.maximum(d, 0.0)
        return jnp.sum(e.reshape(IDX_HEADS, QPAD, e.shape[1]), axis=0)

    for r in range(PPS):
        i_ref[0, :, r * PAGE:(r + 1) * PAGE] = scores(pages[r][0].astype(BF16))

    @pl.when(s == 0)
    def _():
        sn = scores(kin_ref[0])
        qrow = lax.broadcasted_iota(I32, sn.shape, 0)
        kcol = lax.broadcasted_iota(I32, sn.shape, 1)
        inew_ref[0] = jnp.where(kcol <= qrow, sn, -jnp.inf)


def sample_scores(page_table, qi, w, ki_new, cache_idx_k):
    bs, n_pages = page_table.shape
    grid_spec = pltpu.PrefetchScalarGridSpec(
        num_scalar_prefetch=1,
        grid=(bs, n_pages // PPS),
        in_specs=[pl.BlockSpec((1, IDX_HEADS * QPAD, IDX_DIM), lambda b, s, pt: (b, 0, 0)),
                  pl.BlockSpec((1, IDX_HEADS * QPAD, 1), lambda b, s, pt: (b, 0, 0)),
                  pl.BlockSpec((1, PAGE, IDX_DIM), lambda b, s, pt: (b, 0, 0))]
                 + [pl.BlockSpec((1, PAGE, IDX_DIM),
                                 functools.partial(lambda b, s, pt, r: (pt[b * n_pages + s * PPS + r], 0, 0), r=r))
                    for r in range(PPS)],
        out_specs=[pl.BlockSpec((1, QPAD, PPS * PAGE), lambda b, s, pt: (b, 0, s)),
                   pl.BlockSpec((1, QPAD, PAGE), lambda b, s, pt: (b, 0, 0))],
    )
    return pl.pallas_call(
        _sample_scores_kernel,
        grid_spec=grid_spec,
        out_shape=[jax.ShapeDtypeStruct((bs, QPAD, n_pages * PAGE), F32),
                   jax.ShapeDtypeStruct((bs, QPAD, PAGE), F32)],
        compiler_params=_params("parallel", "arbitrary"),
        name="sample_scores",
    )(page_table.reshape(-1), qi, w, ki_new, *([cache_idx_k] * PPS))


def _sample_thr_kernel(i_ref, thr_ref, sc_ref, *, k):
    rows, width = i_ref.shape
    nblk = width // LANES

    def to_keys(c, carry):
        cols = pl.ds(pl.multiple_of(c * LANES, LANES), LANES)
        sc_ref[:, cols] = _order_key(i_ref[:, cols])
        return carry

    lax.fori_loop(0, nblk, to_keys, 0)

    def count_ge(cand):
        def blk(c, acc):
            s = sc_ref[:, pl.ds(pl.multiple_of(c * LANES, LANES), LANES)]
            return acc + jnp.where(s >= cand, 1, 0).astype(I32)
        acc = lax.fori_loop(0, nblk, blk, jnp.zeros((rows, LANES), I32))
        return jnp.sum(acc, axis=1, keepdims=True)

    thr = _kth_largest_key(count_ge, jnp.full((rows, 1), k, I32))
    thr_ref[...] = jnp.broadcast_to(thr, thr_ref.shape)


def sample_threshold(scores, k):
    rows, width = scores.shape
    return pl.pallas_call(
        functools.partial(_sample_thr_kernel, k=k),
        out_shape=jax.ShapeDtypeStruct((rows, LANES), I32),
        scratch_shapes=[pltpu.VMEM((rows, width), I32)],
        compiler_params=pltpu.CompilerParams(vmem_limit_bytes=VMEM_LIMIT),
        name="sample_threshold",
    )(scores)


def _sample_attn_kernel(pt_ref, q_ref, i_ref, inew_ref, thr_ref, kn_ref, vn_ref, *refs):
    kpages, vpages = refs[:PPA], refs[PPA:2 * PPA]
    o_ref, m_ref, l_ref, acc_ref = refs[2 * PPA:]
    s = pl.program_id(1)
    scale = HEAD_DIM ** -0.5
    rows_g = GROUP * QPAD
    thr = thr_ref[0]

    @pl.when(s == 0)
    def _():
        m_ref[...] = jnp.full(m_ref.shape, NEG_BIG, F32)
        l_ref[...] = jnp.zeros(l_ref.shape, F32)
        acc_ref[...] = jnp.zeros(acc_ref.shape, F32)

    def update(scores_blk, k_blk, v_blk):
        sel = _order_key(scores_blk) >= thr
        sel = jnp.concatenate([sel] * GROUP, axis=0)
        for g in range(N_KV_HEADS):
            rs = slice(g * rows_g, (g + 1) * rows_g)
            kg = k_blk[:, g * HEAD_DIM:(g + 1) * HEAD_DIM]
            vg = v_blk[:, g * HEAD_DIM:(g + 1) * HEAD_DIM]
            sc = lax.dot_general(q_ref[0, rs, :], kg, (((1,), (1,)), ((), ())),
                                 preferred_element_type=F32) * scale
            sc = jnp.where(sel, sc, NEG_BIG)
            m_old = m_ref[rs, :]
            m_new = jnp.maximum(m_old, jnp.max(sc, axis=1, keepdims=True))
            p = jnp.where(sel, jnp.exp(sc - m_new), 0.0)
            alpha = jnp.exp(m_old - m_new)
            l_ref[rs, :] = alpha * l_ref[rs, :] + jnp.sum(p, axis=1, keepdims=True)
            acc_ref[rs, :] = alpha * acc_ref[rs, :] + jnp.dot(p.astype(BF16), vg, preferred_element_type=F32)
            m_ref[rs, :] = m_new

    for r in range(PPA):
        update(i_ref[0, :, r * PAGE:(r + 1) * PAGE],
               kpages[r][0].astype(BF16), vpages[r][0].astype(BF16))

    @pl.when(s == pl.num_programs(1) - 1)
    def _():
        update(inew_ref[0], kn_ref[0], vn_ref[0])
        o_ref[0] = (acc_ref[...] / l_ref[...]).astype(o_ref.dtype)


def sample_attention(page_table, q, scores, scores_new, thr, k_new, v_new, cache_k, cache_v):
    bs, n_pages = page_table.shape
    kvw = N_KV_HEADS * HEAD_DIM
    page_spec = lambda r: pl.BlockSpec(
        (1, PAGE, kvw), functools.partial(lambda b, s, pt, r: (pt[b * n_pages + s * PPA + r], 0, 0), r=r))
    fixed = lambda shape: pl.BlockSpec(shape, lambda b, s, pt: (b, 0, 0))
    grid_spec = pltpu.PrefetchScalarGridSpec(
        num_scalar_prefetch=1,
        grid=(bs, n_pages // PPA),
        in_specs=[fixed((1, N_HEADS * QPAD, HEAD_DIM)),
                  pl.BlockSpec((1, QPAD, PPA * PAGE), lambda b, s, pt: (b, 0, s)),
                  fixed((1, QPAD, PAGE)), fixed((1, QPAD, LANES)),
                  fixed((1, PAGE, kvw)), fixed((1, PAGE, kvw))]
                 + [page_spec(r) for r in range(PPA)] + [page_spec(r) for r in range(PPA)],
        out_specs=fixed((1, N_HEADS * QPAD, HEAD_DIM)),
        scratch_shapes=[pltpu.VMEM((N_HEADS * QPAD, 1), F32),
                        pltpu.VMEM((N_HEADS * QPAD, 1), F32),
                        pltpu.VMEM((N_HEADS * QPAD, HEAD_DIM), F32)],
    )
    return pl.pallas_call(
        _sample_attn_kernel,
        grid_spec=grid_spec,
        out_shape=jax.ShapeDtypeStruct((bs, N_HEADS * QPAD, HEAD_DIM), BF16),
        compiler_params=_params("parallel", "arbitrary"),
        name="sample_attn",
    )(page_table.reshape(-1), q, scores, scores_new, thr, k_new, v_new,
      *([cache_k] * PPA), *([cache_v] * PPA))


def _glu(a, gate):
    return a * jax.nn.sigmoid(gate)


def _first(a):
    return a


def _token_tile(n_tok, pref):
    return pref if n_tok % pref == 0 else n_tok


def _layer_front(x2, pos_tabs, n_pos_tiles, wts, tm):
    p = rms_mm(x2, wts["norm_attn"], [wts["w_a"]], _first, F32, tm, 384)
    g = rms_mm(x2, wts["norm_attn"], [wts["w_u1"], wts["w_u2"]], _glu, F32, tm, 512)
    prep = attn_prep(p, pos_tabs, n_pos_tiles, tm)
    return prep, g


def _layer_back(x2, a, c, wts, tm):
    h, n2 = out_proj(a, c, wts["w_oa"], wts["w_oc"], x2, wts["norm_ffn"], min(tm, 256))
    act = ffn_up(n2, wts["w_gate"], wts["w_up"], tm, 512)
    return ffn_down(act, wts["w_down"], h, wts["norm_final"], tm, 512)


def _prep_weights(l, norm_attn, w_in, conv_dw, conv_dw_bias, conv_ln_g, conv_ln_b, w_out, norm_ffn,
                  w_gate, w_up, w_down, norm_final):
    w = w_in[l]
    aw = N_HEADS * HEAD_DIM
    kvw = N_KV_HEADS * HEAD_DIM
    o_small = aw + 2 * kvw + IDX_HEADS * IDX_DIM
    o_u = o_small + IDX_DIM + IDX_HEADS
    cw = (w.shape[1] - o_u) // 2
    pad = jnp.zeros((w.shape[0], LANES - IDX_DIM - IDX_HEADS), w.dtype)
    w_a = jnp.concatenate([w[:, :o_u], pad], axis=1).astype(BF16)
    dw = jnp.concatenate([conv_dw[l], jnp.zeros((HALO - CONV_KERNEL, cw), F32)], axis=0)
    return {
        "norm_attn": norm_attn[l][None, :], "w_a": w_a,
        "w_u1": w[:, o_u:o_u + cw].astype(BF16), "w_u2": w[:, o_u + cw:].astype(BF16),
        "dw": dw, "dw_bias": conv_dw_bias[l][None, :], "ln_g": conv_ln_g[l][None, :], "ln_b": conv_ln_b[l][None, :],
        "w_oa": w_out[l][:aw].astype(BF16), "w_oc": w_out[l][aw:].astype(BF16),
        "norm_ffn": norm_ffn[l][None, :],
        "w_gate": w_gate[l].astype(BF16), "w_up": w_up[l].astype(BF16), "w_down": w_down[l].astype(BF16),
        "norm_final": norm_final[None, :],
    }


def kernel(x_prompt, x_sample, cache_k, cache_v, cache_idx_k, state_conv, page_table, norm_attn, w_in, conv_dw, conv_dw_bias, conv_ln_g, conv_ln_b, w_out, norm_ffn, w_gate, w_up, w_down, norm_final):
    batch, seq, d_model = x_prompt.shape
    bs, ts = x_sample.shape[:2]
    past = page_table.shape[1] * PAGE
    cw = conv_dw.shape[2]
    assert norm_attn.shape[0] == 1, "single-layer step"
    wts = _prep_weights(0, norm_attn, w_in, conv_dw, conv_dw_bias, conv_ln_g, conv_ln_b, w_out, norm_ffn,
                        w_gate, w_up, w_down, norm_final)

    tm = _token_tile(batch * seq, 512)
    xp = x_prompt.reshape(batch * seq, d_model)
    tabs_p = rope_tables(jnp.arange(seq, dtype=jnp.int32))
    (q, kf, kb, vf, vb, qi, kif, kib, wi), g = _layer_front(xp, tabs_p, seq // tm, wts, tm)
    wit = wi.reshape(batch, seq, IDX_HEADS).transpose(0, 2, 1)
    a = prompt_attention(q, kb, vb, qi, kib, wit, batch, seq)
    g3 = g.reshape(batch, seq, cw)
    c = conformer_conv(g3, jnp.zeros((batch, HALO, cw), F32), wts["dw"], wts["dw_bias"], wts["ln_g"], wts["ln_b"],
                       min(seq, 256))
    y_prompt = _layer_back(xp, a, c.reshape(batch * seq, cw), wts, tm).reshape(batch, seq, d_model)
    k_prompt = kf.reshape(1, batch, seq, N_KV_HEADS, HEAD_DIM)
    v_prompt = vf.reshape(1, batch, seq, N_KV_HEADS, HEAD_DIM)
    idx_k_prompt = kif.reshape(1, batch, seq, IDX_DIM)
    conv_prompt = g3[:, seq - (CONV_KERNEL - 1):][None]

    ns = bs * ts
    xs = x_sample.reshape(ns, d_model)
    tabs_s = tuple(jnp.tile(t, (bs, 1)) for t in rope_tables(past + jnp.arange(ts, dtype=jnp.int32)))
    (q, kf, kb, vf, vb, qi, kif, kib, wi), g = _layer_front(xs, tabs_s, 1, wts, ns)

    def pad_q(t):
        return jnp.pad(t, ((0, 0), (0, QPAD - ts)) + ((0, 0),) * (t.ndim - 2))

    def pad_keys(t):
        return jnp.pad(t, ((0, 0), (0, PAGE - ts), (0, 0)))

    qi_s = pad_q(qi.reshape(IDX_HEADS, bs, ts, IDX_DIM).transpose(1, 2, 0, 3))
    qi_s = qi_s.transpose(0, 2, 1, 3).reshape(bs, IDX_HEADS * QPAD, IDX_DIM)
    w_s = pad_q(wi.reshape(bs, ts, IDX_HEADS)).transpose(0, 2, 1).reshape(bs, IDX_HEADS * QPAD, 1)
    sc_past, sc_new = sample_scores(page_table, qi_s, w_s, pad_keys(kib.reshape(bs, ts, IDX_DIM)), cache_idx_k[0])
    sc_all = jnp.concatenate([sc_past[:, :ts], sc_new[:, :ts]], axis=-1).reshape(ns, past + PAGE)
    thr = sample_threshold(sc_all, min(TOPK_MAX, (past + ts) // 4))
    thr = pad_q(thr.reshape(bs, ts, LANES))
    q_s = pad_q(q.reshape(bs, ts, N_HEADS, HEAD_DIM)).transpose(0, 2, 1, 3).reshape(bs, N_HEADS * QPAD, HEAD_DIM)
    kvw = N_KV_HEADS * HEAD_DIM
    o = sample_attention(page_table, q_s, sc_past, sc_new, thr,
                         pad_keys(kb.reshape(bs, ts, kvw)), pad_keys(vb.reshape(bs, ts, kvw)),
                         cache_k[0].reshape(-1, PAGE, kvw), cache_v[0].reshape(-1, PAGE, kvw))
    a = o.reshape(bs, N_HEADS, QPAD, HEAD_DIM)[:, :, :ts].transpose(0, 2, 1, 3).reshape(ns, N_HEADS * HEAD_DIM)

    g3 = g.reshape(bs, ts, cw)
    prev = state_conv[0]
    prev_pad = jnp.pad(prev, ((0, 0), (HALO - (CONV_KERNEL - 1), 0), (0, 0)))
    c = conformer_conv(pad_q(g3), prev_pad, wts["dw"], wts["dw_bias"], wts["ln_g"], wts["ln_b"], QPAD)
    y_sample = _layer_back(xs, a, c[:, :ts].reshape(ns, cw), wts, ns).reshape(bs, ts, d_model)
    k_sample = kf.reshape(1, bs, ts, N_KV_HEADS, HEAD_DIM)
    v_sample = vf.reshape(1, bs, ts, N_KV_HEADS, HEAD_DIM)
    idx_k_sample = kif.reshape(1, bs, ts, IDX_DIM)
    conv_sample = jnp.concatenate([prev, g3], axis=1)[:, ts:][None]

    return (y_prompt, y_sample, k_prompt, v_prompt, idx_k_prompt, conv_prompt,
            k_sample, v_sample, idx_k_sample, conv_sample)
```

```python
import functools

import numpy as np
import jax
import jax.numpy as jnp
from jax import lax
from jax.experimental import pallas as pl
from jax.experimental.pallas import tpu as pltpu

N_HEADS = 8
HEAD_DIM = 128
N_KV_HEADS = 2
GROUP = N_HEADS // N_KV_HEADS
IDX_HEADS = 16
IDX_DIM = 64
TOPK_MAX = 256
ROPE_THETA = 10000.0
CONV_KERNEL = 31
EPS = 1e-6
PAGE = 128

LANES = 128
SUBLANES = 8
HALO = 32
VMEM_LIMIT = 56 * 1024 * 1024
INT_MIN = np.int32(-2 ** 31)
NEG_BIG = -1e30
LOG2E = 1.4426950408889634

BF16 = jnp.bfloat16
F32 = jnp.float32
I32 = jnp.int32


def _params(*sem):
    return pltpu.CompilerParams(dimension_semantics=sem, vmem_limit_bytes=VMEM_LIMIT)


def _rank_to_float(rank):
    key = rank ^ INT_MIN
    return pltpu.bitcast(key ^ ((key >> 31) & np.int32(0x7FFFFFFF)), F32)


def _kth_largest(count_ge, k):
    def body(it, acc):
        bit = lax.shift_left(np.int32(1), np.int32(31) - it)
        cand = acc | bit
        return jnp.where(count_ge(_rank_to_float(cand)) >= k, cand, acc)
    acc = lax.fori_loop(0, 32, body, jnp.zeros_like(k))
    return _rank_to_float(acc)


def _rms_mm_kernel(x_ref, g_ref, *refs, n_w, combine):
    w_refs, o_ref, n_ref = refs[:n_w], refs[n_w], refs[n_w + 1]

    @pl.when(pl.program_id(1) == 0)
    def _():
        x = x_ref[...]
        y = x * lax.rsqrt(jnp.mean(x * x, axis=-1, keepdims=True) + EPS)
        n_ref[...] = (y * g_ref[...]).astype(BF16)

    n = n_ref[...]
    outs = [jnp.dot(n, w[...], preferred_element_type=F32) for w in w_refs]
    o_ref[...] = combine(*outs).astype(o_ref.dtype)


def rms_mm(x, g, ws, combine, out_dtype, tm, tn):
    n_tok, d = x.shape
    f = ws[0].shape[1]
    kern = functools.partial(_rms_mm_kernel, n_w=len(ws), combine=combine)
    return pl.pallas_call(
        kern,
        grid=(n_tok // tm, f // tn),
        in_specs=[pl.BlockSpec((tm, d), lambda i, j: (i, 0)),
                  pl.BlockSpec((1, d), lambda i, j: (0, 0))]
                 + [pl.BlockSpec((d, tn), lambda i, j: (0, j)) for _ in ws],
        out_specs=pl.BlockSpec((tm, tn), lambda i, j: (i, j)),
        out_shape=jax.ShapeDtypeStruct((n_tok, f), out_dtype),
        scratch_shapes=[pltpu.VMEM((tm, d), BF16)],
        compiler_params=_params("parallel", "arbitrary"),
        name="rms_mm",
    )(x, g, *ws)


def _dual_mm_kernel(n_ref, w1_ref, w2_ref, o_ref):
    n = n_ref[...]
    a = jnp.dot(n, w1_ref[...], preferred_element_type=F32)
    b = jnp.dot(n, w2_ref[...], preferred_element_type=F32)
    o_ref[...] = (a * jax.nn.sigmoid(a) * b).astype(o_ref.dtype)


def ffn_up(n, w_gate, w_up, tm, tn):
    n_tok, d = n.shape
    f = w_gate.shape[1]
    return pl.pallas_call(
        _dual_mm_kernel,
        grid=(n_tok // tm, f // tn),
        in_specs=[pl.BlockSpec((tm, d), lambda i, j: (i, 0)),
                  pl.BlockSpec((d, tn), lambda i, j: (0, j)),
                  pl.BlockSpec((d, tn), lambda i, j: (0, j))],
        out_specs=pl.BlockSpec((tm, tn), lambda i, j: (i, j)),
        out_shape=jax.ShapeDtypeStruct((n_tok, f), BF16),
        compiler_params=_params("parallel", "arbitrary"),
        name="ffn_up",
    )(n, w_gate, w_up)


def _out_proj_kernel(a_ref, c_ref, wa_ref, wc_ref, x_ref, g_ref, h_ref, n2_ref):
    h = x_ref[...]
    h = h + jnp.dot(a_ref[...], wa_ref[...], preferred_element_type=F32)
    h = h + jnp.dot(c_ref[...], wc_ref[...], preferred_element_type=F32)
    h_ref[...] = h
    y = h * lax.rsqrt(jnp.mean(h * h, axis=-1, keepdims=True) + EPS)
    n2_ref[...] = (y * g_ref[...]).astype(BF16)


def out_proj(a, c, wa, wc, x, g, tm):
    n_tok, d = x.shape
    ka, kc = a.shape[1], c.shape[1]
    return pl.pallas_call(
        _out_proj_kernel,
        grid=(n_tok // tm,),
        in_specs=[pl.BlockSpec((tm, ka), lambda i: (i, 0)),
                  pl.BlockSpec((tm, kc), lambda i: (i, 0)),
                  pl.BlockSpec((ka, d), lambda i: (0, 0)),
                  pl.BlockSpec((kc, d), lambda i: (0, 0)),
                  pl.BlockSpec((tm, d), lambda i: (i, 0)),
                  pl.BlockSpec((1, d), lambda i: (0, 0))],
        out_specs=[pl.BlockSpec((tm, d), lambda i: (i, 0)),
                   pl.BlockSpec((tm, d), lambda i: (i, 0))],
        out_shape=[jax.ShapeDtypeStruct((n_tok, d), F32),
                   jax.ShapeDtypeStruct((n_tok, d), BF16)],
        compiler_params=_params("parallel"),
        name="out_proj",
    )(a, c, wa, wc, x, g)


def _down_kernel(a_ref, w_ref, h_ref, g_ref, y_ref, acc_ref):
    k = pl.program_id(1)

    @pl.when(k == 0)
    def _():
        acc_ref[...] = h_ref[...]

    acc_ref[...] += jnp.dot(a_ref[...], w_ref[...], preferred_element_type=F32)

    @pl.when(k == pl.num_programs(1) - 1)
    def _():
        h = acc_ref[...]
        y = h * lax.rsqrt(jnp.mean(h * h, axis=-1, keepdims=True) + EPS)
        y_ref[...] = y * g_ref[...]


def ffn_down(act, w_down, h, g, tm, tk):
    n_tok, f = act.shape
    d = w_down.shape[1]
    return pl.pallas_call(
        _down_kernel,
        grid=(n_tok // tm, f // tk),
        in_specs=[pl.BlockSpec((tm, tk), lambda i, k: (i, k)),
                  pl.BlockSpec((tk, d), lambda i, k: (k, 0)),
                  pl.BlockSpec((tm, d), lambda i, k: (i, 0)),
                  pl.BlockSpec((1, d), lambda i, k: (0, 0))],
        out_specs=pl.BlockSpec((tm, d), lambda i, k: (i, 0)),
        out_shape=jax.ShapeDtypeStruct((n_tok, d), F32),
        scratch_shapes=[pltpu.VMEM((tm, d), F32)],
        compiler_params=_params("parallel", "arbitrary"),
        name="ffn_down",
    )(act, w_down, h, g)


def _rope128(x, cos, sin):
    return x * cos + pltpu.roll(x, 64, 1) * sin


def _rope64(x, cos, sin, lane):
    partner = jnp.where((lane & 63) < 32, pltpu.roll(x, 96, 1), pltpu.roll(x, 32, 1))
    return x * cos + partner * sin


def _attn_prep_kernel(p_ref, c128_ref, s128_ref, c64_ref, s64_ref,
                      q_ref, kf_ref, kb_ref, vf_ref, vb_ref, qi_ref, kif_ref, kib_ref, wi_ref):
    c128, s128 = c128_ref[...], s128_ref[...]
    c64, s64 = c64_ref[...], s64_ref[...]
    tm = p_ref.shape[0]
    lane = lax.broadcasted_iota(I32, (tm, LANES), 1)
    for h in range(N_HEADS):
        x = p_ref[:, h * 128:(h + 1) * 128]
        q_ref[:, h * 128:(h + 1) * 128] = _rope128(x, c128, s128).astype(BF16)
    for h in range(N_KV_HEADS):
        x = p_ref[:, 1024 + h * 128:1024 + (h + 1) * 128]
        r = _rope128(x, c128, s128)
        kf_ref[:, h * 128:(h + 1) * 128] = r
        kb_ref[:, h * 128:(h + 1) * 128] = r.astype(BF16)
    v = p_ref[:, 1280:1536]
    vf_ref[...] = v
    vb_ref[...] = v.astype(BF16)
    for hp in range(IDX_HEADS // 2):
        x = p_ref[:, 1536 + hp * 128:1536 + (hp + 1) * 128]
        r = _rope64(x, c64, s64, lane)
        qi_ref[2 * hp] = r[:, :64].astype(BF16)
        qi_ref[2 * hp + 1] = r[:, 64:].astype(BF16)
    x = p_ref[:, 2560:2688]
    r = _rope64(x, c64, s64, lane)
    kif_ref[...] = r[:, :64]
    kib_ref[...] = r[:, :64].astype(BF16)
    wi_ref[...] = x[:, 64:64 + IDX_HEADS] * (IDX_HEADS ** -0.5 * IDX_DIM ** -0.5)


def attn_prep(p, tabs, n_pos_tiles, tm):
    n_tok = p.shape[0]
    tab_spec = pl.BlockSpec((tm, LANES), lambda i: (i % n_pos_tiles, 0))
    row = lambda w: pl.BlockSpec((tm, w), lambda i: (i, 0))
    sds = jax.ShapeDtypeStruct
    return pl.pallas_call(
        _attn_prep_kernel,
        grid=(n_tok // tm,),
        in_specs=[pl.BlockSpec((tm, p.shape[1]), lambda i: (i, 0))] + [tab_spec] * 4,
        out_specs=[row(1024), row(256), row(256), row(256), row(256),
                   pl.BlockSpec((IDX_HEADS, tm, IDX_DIM), lambda i: (0, i, 0)),
                   row(IDX_DIM), row(IDX_DIM), row(IDX_HEADS)],
        out_shape=[sds((n_tok, 1024), BF16), sds((n_tok, 256), F32), sds((n_tok, 256), BF16),
                   sds((n_tok, 256), F32), sds((n_tok, 256), BF16),
                   sds((IDX_HEADS, n_tok, IDX_DIM), BF16),
                   sds((n_tok, IDX_DIM), F32), sds((n_tok, IDX_DIM), BF16),
                   sds((n_tok, IDX_HEADS), F32)],
        compiler_params=_params("parallel"),
        name="attn_prep",
    )(p, *tabs)


def rope_tables(pos):
    pos = pos.astype(F32)[:, None]

    def tab(half):
        freqs = ROPE_THETA ** (-jnp.arange(half, dtype=F32) / half)
        ang = pos * freqs[None, :]
        c, s = jnp.cos(ang), jnp.sin(ang)
        reps = LANES // (2 * half)
        return jnp.tile(jnp.concatenate([c, c], -1), (1, reps)), jnp.tile(jnp.concatenate([-s, s], -1), (1, reps))

    c128, s128 = tab(HEAD_DIM // 2)
    c64, s64 = tab(IDX_DIM // 2)
    return c128, s128, c64, s64


def _prompt_attn_kernel(q_ref, k_ref, v_ref, qi_ref, ki_ref, wit_ref, o_ref,
                        sc_ref, qs_ref, m_ref, l_ref, acc_ref, *, topk, kc, cpb):
    i = pl.program_id(1)
    tq = q_ref.shape[1]
    nchunk = ((i + 1) * tq + kc - 1) // kc
    gq = GROUP * tq

    def chunk_rows(c):
        return pl.ds(pl.multiple_of(c * kc, kc), kc)

    qi_all = qi_ref[...].reshape(IDX_HEADS * tq, IDX_DIM)
    qpos = i * tq + lax.broadcasted_iota(I32, (kc, tq), 1)

    def score_chunk(c, carry):
        d = lax.dot_general(ki_ref[0, chunk_rows(c), :], qi_all, (((1,), (1,)), ((), ())),
                            preferred_element_type=F32)
        tot = jnp.zeros((kc, tq), F32)
        for h in range(IDX_HEADS):
            tot = tot + wit_ref[0, h:h + 1, :] * jnp.maximum(d[:, h * tq:(h + 1) * tq], 0.0)
        kpos = c * kc + lax.broadcasted_iota(I32, (kc, tq), 0)
        sc_ref[chunk_rows(c), :] = jnp.where(kpos <= qpos, tot, -jnp.inf)
        return carry

    lax.fori_loop(0, nchunk, score_chunk, 0)

    kq = jnp.minimum(topk, i * tq + 1 + lax.broadcasted_iota(I32, (1, tq), 1))
    nacc = 4 * 8

    @pl.when(nchunk % cpb != 0)
    def _():
        sc_ref[chunk_rows(nchunk), :] = jnp.full((kc, tq), -jnp.inf, F32)

    def count_ge(cand):
        def blk(c, acc):
            rows = pl.ds(pl.multiple_of(c * (cpb * kc), cpb * kc), cpb * kc)
            hit = jnp.where(sc_ref[rows, :] >= cand, 1, 0).astype(I32)
            return acc + jnp.sum(hit.reshape(cpb * kc // nacc, nacc, tq), axis=0)
        acc = lax.fori_loop(0, (nchunk + cpb - 1) // cpb, blk, jnp.zeros((nacc, tq), I32))
        return jnp.sum(acc, axis=0, keepdims=True)

    thr = _kth_largest(count_ge, kq)

    for h in range(N_HEADS):
        g, hh = divmod(h, GROUP)
        qs_ref[g, hh * tq:(hh + 1) * tq, :] = q_ref[0, :, h * HEAD_DIM:(h + 1) * HEAD_DIM]
    m_ref[...] = jnp.full(m_ref.shape, NEG_BIG, F32)
    l_ref[...] = jnp.zeros(l_ref.shape, F32)
    acc_ref[...] = jnp.zeros(acc_ref.shape, F32)
    c2 = HEAD_DIM ** -0.5 * LOG2E

    def attn_chunk(c, carry):
        rows = chunk_rows(c)
        bias_t = jnp.where(sc_ref[rows, :] >= thr, 0.0, NEG_BIG).astype(F32)
        bias = jnp.transpose(bias_t)
        for g in range(N_KV_HEADS):
            kg = k_ref[0, rows, g * HEAD_DIM:(g + 1) * HEAD_DIM]
            vg = v_ref[0, rows, g * HEAD_DIM:(g + 1) * HEAD_DIM]
            s = lax.dot_general(qs_ref[g], kg, (((1,), (1,)), ((), ())), preferred_element_type=F32)
            s = (s.reshape(GROUP, tq, kc) + bias[None]).reshape(gq, kc)
            m_old = m_ref[g]
            m_new = jnp.maximum(m_old, jnp.max(s, axis=1, keepdims=True))
            p = jnp.exp2((s - jnp.tile(m_new, (1, kc // LANES))) * c2)
            alpha = jnp.exp2((m_old - m_new) * c2)
            l_ref[g] = alpha * l_ref[g] + jnp.sum(p, axis=1, keepdims=True)
            acc_ref[g] = alpha * acc_ref[g] + jnp.dot(p.astype(BF16), vg, preferred_element_type=F32)
            m_ref[g] = m_new
        return carry

    lax.fori_loop(0, nchunk, attn_chunk, 0)

    for h in range(N_HEADS):
        g, hh = divmod(h, GROUP)
        rs = slice(hh * tq, (hh + 1) * tq)
        o_ref[0, :, h * HEAD_DIM:(h + 1) * HEAD_DIM] = (acc_ref[g, rs, :] / l_ref[g, rs, :]).astype(o_ref.dtype)


def prompt_attention(q, k, v, qi, ki, wit, batch, seq, tq=LANES):
    nt = seq // tq
    kc = min(seq, 512)
    q3 = q.reshape(batch, seq, N_HEADS * HEAD_DIM)
    k3 = k.reshape(batch, seq, N_KV_HEADS * HEAD_DIM)
    v3 = v.reshape(batch, seq, N_KV_HEADS * HEAD_DIM)
    ki3 = ki.reshape(batch, seq, IDX_DIM)
    out = pl.pallas_call(
        functools.partial(_prompt_attn_kernel, topk=min(TOPK_MAX, seq // 4), kc=kc,
                          cpb=2 if seq % (2 * kc) == 0 else 1),
        grid=(batch, nt),
        in_specs=[pl.BlockSpec((1, tq, N_HEADS * HEAD_DIM), lambda b, i: (b, i, 0)),
                  pl.BlockSpec((1, seq, N_KV_HEADS * HEAD_DIM), lambda b, i: (b, 0, 0)),
                  pl.BlockSpec((1, seq, N_KV_HEADS * HEAD_DIM), lambda b, i: (b, 0, 0)),
                  pl.BlockSpec((IDX_HEADS, tq, IDX_DIM), lambda b, i: (0, b * nt + i, 0)),
                  pl.BlockSpec((1, seq, IDX_DIM), lambda b, i: (b, 0, 0)),
                  pl.BlockSpec((1, IDX_HEADS, tq), lambda b, i: (b, 0, i))],
        out_specs=pl.BlockSpec((1, tq, N_HEADS * HEAD_DIM), lambda b, i: (b, i, 0)),
        out_shape=jax.ShapeDtypeStruct((batch, seq, N_HEADS * HEAD_DIM), BF16),
        scratch_shapes=[pltpu.VMEM((seq, tq), F32),
                        pltpu.VMEM((N_KV_HEADS, GROUP * tq, HEAD_DIM), BF16),
                        pltpu.VMEM((N_KV_HEADS, GROUP * tq, LANES), F32),
                        pltpu.VMEM((N_KV_HEADS, GROUP * tq, LANES), F32),
                        pltpu.VMEM((N_KV_HEADS, GROUP * tq, HEAD_DIM), F32)],
        compiler_params=_params("parallel", "arbitrary"),
        name="prompt_attn",
    )(q3, k3, v3, qi, ki3, wit)
    return out.reshape(batch * seq, N_HEADS * HEAD_DIM)


def _conv_kernel(g_ref, gh_ref, prev_ref, dw_ref, b_ref, lg_ref, lb_ref, o_ref, buf_ref, sh_ref, y_ref):
    i = pl.program_id(1)
    tt, ch = g_ref.shape[1], g_ref.shape[2]
    buf_ref[HALO:, :] = g_ref[0]

    @pl.when(i == 0)
    def _():
        buf_ref[:HALO, :] = prev_ref[0]

    @pl.when(i > 0)
    def _():
        buf_ref[:HALO, :] = gh_ref[0]

    off = HALO - (CONV_KERNEL - 1)
    span = sh_ref.shape[1]
    for r in range(1, SUBLANES):
        sh_ref[r - 1] = buf_ref[r:r + span, :]
    rb = min(tt, 64)
    for cb in range(ch // LANES):
        lanes = slice(cb * LANES, (cb + 1) * LANES)
        w = [dw_ref[j:j + 1, lanes] for j in range(CONV_KERNEL)]
        for r0 in range(0, tt, rb):
            acc = jnp.zeros((rb, LANES), F32)
            for j in range(CONV_KERNEL):
                r = (off + j) % SUBLANES
                q = off + j - r + r0
                src = buf_ref if r == 0 else sh_ref.at[r - 1]
                acc = acc + src[q:q + rb, lanes] * w[j]
            y_ref[r0:r0 + rb, lanes] = acc
    y = y_ref[...] + b_ref[...]
    mu = jnp.mean(y, axis=-1, keepdims=True)
    yc = y - mu
    z = yc * lax.rsqrt(jnp.mean(yc * yc, axis=-1, keepdims=True) + EPS)
    z = z * lg_ref[...] + lb_ref[...]
    o_ref[0] = (z * jax.nn.sigmoid(z)).astype(o_ref.dtype)


def conformer_conv(g, prev, dw, bias, ln_g, ln_b, tt):
    batch, seq, ch = g.shape
    per = tt // HALO
    halo_src = g if seq > tt else prev
    return pl.pallas_call(
        _conv_kernel,
        grid=(batch, seq // tt),
        in_specs=[pl.BlockSpec((1, tt, ch), lambda b, i: (b, i, 0)),
                  pl.BlockSpec((1, HALO, ch), lambda b, i: (b, jnp.maximum(i * per - 1, 0), 0)),
                  pl.BlockSpec((1, HALO, ch), lambda b, i: (b, 0, 0)),
                  pl.BlockSpec((HALO, ch), lambda b, i: (0, 0)),
                  pl.BlockSpec((1, ch), lambda b, i: (0, 0)),
                  pl.BlockSpec((1, ch), lambda b, i: (0, 0)),
                  pl.BlockSpec((1, ch), lambda b, i: (0, 0))],
        out_specs=pl.BlockSpec((1, tt, ch), lambda b, i: (b, i, 0)),
        out_shape=jax.ShapeDtypeStruct((batch, seq, ch), BF16),
        scratch_shapes=[pltpu.VMEM((HALO + tt, ch), F32),
                        pltpu.VMEM((SUBLANES - 1, HALO + tt - SUBLANES, ch), F32),
                        pltpu.VMEM((tt, ch), F32)],
        compiler_params=_params("parallel", "arbitrary"),
        name="conformer_conv",
    )(g, halo_src, prev, dw, bias, ln_g, ln_b)


PPS = 16
PPA = 16
QPAD = 8


def _sample_scores_kernel(pt_ref, qi_ref, w_ref, kin_ref, *refs):
    pages, (i_ref, inew_ref) = refs[:PPS], refs[PPS:]
    s = pl.program_id(1)
    qi = qi_ref[0]
    w = w_ref[0]

    def scores(keys_t):
        d = jnp.dot(qi, keys_t, preferred_element_type=F32)
        e = w * jnp.maximum(d, 0.0)
        return jnp.sum(e.reshape(IDX_HEADS, QPAD, e.shape[1]), axis=0)

    i_ref[0] = scores(jnp.concatenate([p[0].astype(BF16) for p in pages], axis=1))

    @pl.when(s == 0)
    def _():
        sn = scores(kin_ref[0])
        qrow = lax.broadcasted_iota(I32, sn.shape, 0)
        kcol = lax.broadcasted_iota(I32, sn.shape, 1)
        inew_ref[0] = jnp.where(kcol <= qrow, sn, -jnp.inf)


def sample_scores(page_table, qi, w, ki_new_t, cache_idx_kt):
    bs, n_pages = page_table.shape
    grid_spec = pltpu.PrefetchScalarGridSpec(
        num_scalar_prefetch=1,
        grid=(bs, n_pages // PPS),
        in_specs=[pl.BlockSpec((1, IDX_HEADS * QPAD, IDX_DIM), lambda b, s, pt: (b, 0, 0)),
                  pl.BlockSpec((1, IDX_HEADS * QPAD, 1), lambda b, s, pt: (b, 0, 0)),
                  pl.BlockSpec((1, IDX_DIM, PAGE), lambda b, s, pt: (b, 0, 0))]
                 + [pl.BlockSpec((1, IDX_DIM, PAGE),
                                 functools.partial(lambda b, s, pt, r: (pt[b * n_pages + s * PPS + r], 0, 0), r=r))
                    for r in range(PPS)],
        out_specs=[pl.BlockSpec((1, QPAD, PPS * PAGE), lambda b, s, pt: (b, 0, s)),
                   pl.BlockSpec((1, QPAD, PAGE), lambda b, s, pt: (b, 0, 0))],
    )
    return pl.pallas_call(
        _sample_scores_kernel,
        grid_spec=grid_spec,
        out_shape=[jax.ShapeDtypeStruct((bs, QPAD, n_pages * PAGE), F32),
                   jax.ShapeDtypeStruct((bs, QPAD, PAGE), F32)],
        compiler_params=_params("parallel", "arbitrary"),
        name="sample_scores",
    )(page_table.reshape(-1), qi, w, ki_new_t, *([cache_idx_kt] * PPS))


def _sample_thr_kernel(i_ref, thr_ref, *, k):
    rows, width = i_ref.shape
    nblk = width // LANES

    def count_ge(cand):
        def blk(c, acc):
            s = i_ref[:, pl.ds(pl.multiple_of(c * LANES, LANES), LANES)]
            return acc + jnp.where(s >= cand, 1, 0).astype(I32)
        acc = lax.fori_loop(0, nblk, blk, jnp.zeros((rows, LANES), I32))
        return jnp.sum(acc, axis=1, keepdims=True)

    thr = _kth_largest(count_ge, jnp.full((rows, 1), k, I32))
    thr_ref[...] = jnp.broadcast_to(thr, thr_ref.shape)


def sample_threshold(scores, k):
    rows, width = scores.shape
    return pl.pallas_call(
        functools.partial(_sample_thr_kernel, k=k),
        out_shape=jax.ShapeDtypeStruct((rows, LANES), F32),
        compiler_params=pltpu.CompilerParams(vmem_limit_bytes=VMEM_LIMIT),
        name="sample_threshold",
    )(scores)


def _sample_attn_kernel(pt_ref, q_ref, i_ref, inew_ref, thr_ref, kn_ref, vn_ref, *refs):
    kpages, vpages = refs[:PPA], refs[PPA:2 * PPA]
    o_ref, m_ref, l_ref, acc_ref = refs[2 * PPA:]
    s = pl.program_id(1)
    c2 = HEAD_DIM ** -0.5 * LOG2E
    rows_g = GROUP * QPAD
    thr = thr_ref[0]

    @pl.when(s == 0)
    def _():
        m_ref[...] = jnp.full(m_ref.shape, NEG_BIG, F32)
        l_ref[...] = jnp.zeros(l_ref.shape, F32)
        acc_ref[...] = jnp.zeros(acc_ref.shape, F32)

    def update(scores_blk, kv_of_group):
        nk = scores_blk.shape[1]
        bias = jnp.where(scores_blk >= jnp.tile(thr, (1, nk // LANES)), 0.0, NEG_BIG).astype(F32)
        for g in range(N_KV_HEADS):
            rs = slice(g * rows_g, (g + 1) * rows_g)
            kg, vg = kv_of_group(g)
            sc = lax.dot_general(q_ref[0, rs, :], kg, (((1,), (1,)), ((), ())), preferred_element_type=F32)
            sc = (sc.reshape(GROUP, QPAD, nk) + bias[None]).reshape(rows_g, nk)
            m_old = m_ref[rs, :]
            m_new = jnp.maximum(m_old, jnp.max(sc, axis=1, keepdims=True))
            p = jnp.exp2((sc - jnp.tile(m_new, (1, nk // LANES))) * c2)
            alpha = jnp.exp2((m_old - m_new) * c2)
            l_ref[rs, :] = alpha * l_ref[rs, :] + jnp.sum(p, axis=1, keepdims=True)
            acc_ref[rs, :] = alpha * acc_ref[rs, :] + jnp.dot(p.astype(BF16), vg, preferred_element_type=F32)
            m_ref[rs, :] = m_new

    def paged_kv(g):
        rows = pl.ds(g, PAGE, stride=N_KV_HEADS)
        kg = jnp.concatenate([p[rows, :].astype(BF16) for p in kpages], axis=0)
        vg = jnp.concatenate([p[rows, :].astype(BF16) for p in vpages], axis=0)
        return kg, vg

    update(i_ref[0], paged_kv)

    @pl.when(s == pl.num_programs(1) - 1)
    def _():
        update(inew_ref[0], lambda g: (kn_ref[0, :, g * HEAD_DIM:(g + 1) * HEAD_DIM],
                                       vn_ref[0, :, g * HEAD_DIM:(g + 1) * HEAD_DIM]))
        o_ref[0] = (acc_ref[...] / l_ref[...]).astype(o_ref.dtype)


def sample_attention(page_table, q, scores, scores_new, thr, k_new, v_new, cache_k_rows, cache_v_rows):
    bs, n_pages = page_table.shape
    kvw = N_KV_HEADS * HEAD_DIM
    page_rows = PAGE * N_KV_HEADS
    page_spec = lambda r: pl.BlockSpec(
        (page_rows, HEAD_DIM), functools.partial(lambda b, s, pt, r: (pt[b * n_pages + s * PPA + r], 0), r=r))
    fixed = lambda shape: pl.BlockSpec(shape, lambda b, s, pt: (b, 0, 0))
    grid_spec = pltpu.PrefetchScalarGridSpec(
        num_scalar_prefetch=1,
        grid=(bs, n_pages // PPA),
        in_specs=[fixed((1, N_HEADS * QPAD, HEAD_DIM)),
                  pl.BlockSpec((1, QPAD, PPA * PAGE), lambda b, s, pt: (b, 0, s)),
                  fixed((1, QPAD, PAGE)), fixed((1, QPAD, LANES)),
                  fixed((1, PAGE, kvw)), fixed((1, PAGE, kvw))]
                 + [page_spec(r) for r in range(PPA)] + [page_spec(r) for r in range(PPA)],
        out_specs=fixed((1, N_HEADS * QPAD, HEAD_DIM)),
        scratch_shapes=[pltpu.VMEM((N_HEADS * QPAD, LANES), F32),
                        pltpu.VMEM((N_HEADS * QPAD, LANES), F32),
                        pltpu.VMEM((N_HEADS * QPAD, HEAD_DIM), F32)],
    )
    return pl.pallas_call(
        _sample_attn_kernel,
        grid_spec=grid_spec,
        out_shape=jax.ShapeDtypeStruct((bs, N_HEADS * QPAD, HEAD_DIM), BF16),
        compiler_params=_params("parallel", "arbitrary"),
        name="sample_attn",
    )(page_table.reshape(-1), q, scores, scores_new, thr, k_new, v_new,
      *([cache_k_rows] * PPA), *([cache_v_rows] * PPA))


def _glu(a, gate):
    return a * jax.nn.sigmoid(gate)


def _first(a):
    return a


def _token_tile(n_tok, pref):
    return pref if n_tok % pref == 0 else n_tok


def _layer_front(x2, pos_tabs, n_pos_tiles, wts, tm):
    p = rms_mm(x2, wts["norm_attn"], [wts["w_a"]], _first, F32, tm, 896)
    g = rms_mm(x2, wts["norm_attn"], [wts["w_u1"], wts["w_u2"]], _glu, F32, tm, 512)
    prep = attn_prep(p, pos_tabs, n_pos_tiles, tm)
    return prep, g


def _layer_back(x2, a, c, wts, tm):
    h, n2 = out_proj(a, c, wts["w_oa"], wts["w_oc"], x2, wts["norm_ffn"], min(tm, 256))
    act = ffn_up(n2, wts["w_gate"], wts["w_up"], tm, 512)
    return ffn_down(act, wts["w_down"], h, wts["norm_final"], tm, 1408)


def _prep_weights(l, norm_attn, w_in, conv_dw, conv_dw_bias, conv_ln_g, conv_ln_b, w_out, norm_ffn,
                  w_gate, w_up, w_down, norm_final):
    w = w_in[l]
    aw = N_HEADS * HEAD_DIM
    kvw = N_KV_HEADS * HEAD_DIM
    o_small = aw + 2 * kvw + IDX_HEADS * IDX_DIM
    o_u = o_small + IDX_DIM + IDX_HEADS
    cw = (w.shape[1] - o_u) // 2
    pad = jnp.zeros((w.shape[0], LANES - IDX_DIM - IDX_HEADS), w.dtype)
    w_a = jnp.concatenate([w[:, :o_u], pad], axis=1).astype(BF16)
    dw = jnp.concatenate([conv_dw[l], jnp.zeros((HALO - CONV_KERNEL, cw), F32)], axis=0)
    return {
        "norm_attn": norm_attn[l][None, :], "w_a": w_a,
        "w_u1": w[:, o_u:o_u + cw].astype(BF16), "w_u2": w[:, o_u + cw:].astype(BF16),
        "dw": dw, "dw_bias": conv_dw_bias[l][None, :], "ln_g": conv_ln_g[l][None, :], "ln_b": conv_ln_b[l][None, :],
        "w_oa": w_out[l][:aw].astype(BF16), "w_oc": w_out[l][aw:].astype(BF16),
        "norm_ffn": norm_ffn[l][None, :],
        "w_gate": w_gate[l].astype(BF16), "w_up": w_up[l].astype(BF16), "w_down": w_down[l].astype(BF16),
        "norm_final": norm_final[None, :],
    }


def kernel(x_prompt, x_sample, cache_k, cache_v, cache_idx_k, state_conv, page_table, norm_attn, w_in, conv_dw, conv_dw_bias, conv_ln_g, conv_ln_b, w_out, norm_ffn, w_gate, w_up, w_down, norm_final):
    batch, seq, d_model = x_prompt.shape
    bs, ts = x_sample.shape[:2]
    past = page_table.shape[1] * PAGE
    cw = conv_dw.shape[2]
    assert norm_attn.shape[0] == 1, "single-layer step"
    wts = _prep_weights(0, norm_attn, w_in, conv_dw, conv_dw_bias, conv_ln_g, conv_ln_b, w_out, norm_ffn,
                        w_gate, w_up, w_down, norm_final)

    tm = _token_tile(batch * seq, 512)
    xp = x_prompt.reshape(batch * seq, d_model)
    tabs_p = rope_tables(jnp.arange(seq, dtype=jnp.int32))
    (q, kf, kb, vf, vb, qi, kif, kib, wi), g = _layer_front(xp, tabs_p, seq // tm, wts, tm)
    wit = wi.reshape(batch, seq, IDX_HEADS).transpose(0, 2, 1)
    a = prompt_attention(q, kb, vb, qi, kib, wit, batch, seq)
    g3 = g.reshape(batch, seq, cw)
    c = conformer_conv(g3, jnp.zeros((batch, HALO, cw), F32), wts["dw"], wts["dw_bias"], wts["ln_g"], wts["ln_b"],
                       min(seq, 256))
    y_prompt = _layer_back(xp, a, c.reshape(batch * seq, cw), wts, tm).reshape(batch, seq, d_model)
    k_prompt = kf.reshape(1, batch, seq, N_KV_HEADS, HEAD_DIM)
    v_prompt = vf.reshape(1, batch, seq, N_KV_HEADS, HEAD_DIM)
    idx_k_prompt = kif.reshape(1, batch, seq, IDX_DIM)
    conv_prompt = g3[:, seq - (CONV_KERNEL - 1):][None]

    ns = bs * ts
    xs = x_sample.reshape(ns, d_model)
    tabs_s = tuple(jnp.tile(t, (bs, 1)) for t in rope_tables(past + jnp.arange(ts, dtype=jnp.int32)))
    (q, kf, kb, vf, vb, qi, kif, kib, wi), g = _layer_front(xs, tabs_s, 1, wts, ns)

    def pad_q(t):
        return jnp.pad(t, ((0, 0), (0, QPAD - ts)) + ((0, 0),) * (t.ndim - 2))

    def pad_keys(t):
        return jnp.pad(t, ((0, 0), (0, PAGE - ts), (0, 0)))

    qi_s = pad_q(qi.reshape(IDX_HEADS, bs, ts, IDX_DIM).transpose(1, 2, 0, 3))
    qi_s = qi_s.transpose(0, 2, 1, 3).reshape(bs, IDX_HEADS * QPAD, IDX_DIM)
    w_s = pad_q(wi.reshape(bs, ts, IDX_HEADS)).transpose(0, 2, 1).reshape(bs, IDX_HEADS * QPAD, 1)
    ki_new_t = jnp.swapaxes(pad_keys(kib.reshape(bs, ts, IDX_DIM)), 1, 2)
    sc_past, sc_new = sample_scores(page_table, qi_s, w_s, ki_new_t, jnp.swapaxes(cache_idx_k[0], 1, 2))
    sc_all = jnp.concatenate([sc_past[:, :ts], sc_new[:, :ts]], axis=-1).reshape(ns, past + PAGE)
    thr = sample_threshold(sc_all, min(TOPK_MAX, (past + ts) // 4))
    thr = pad_q(thr.reshape(bs, ts, LANES))
    q_s = pad_q(q.reshape(bs, ts, N_HEADS, HEAD_DIM)).transpose(0, 2, 1, 3).reshape(bs, N_HEADS * QPAD, HEAD_DIM)
    kvw = N_KV_HEADS * HEAD_DIM
    o = sample_attention(page_table, q_s, sc_past, sc_new, thr,
                         pad_keys(kb.reshape(bs, ts, kvw)), pad_keys(vb.reshape(bs, ts, kvw)),
                         cache_k[0].reshape(-1, HEAD_DIM), cache_v[0].reshape(-1, HEAD_DIM))
    a = o.reshape(bs, N_HEADS, QPAD, HEAD_DIM)[:, :, :ts].transpose(0, 2, 1, 3).reshape(ns, N_HEADS * HEAD_DIM)

    g3 = g.reshape(bs, ts, cw)
    prev = state_conv[0]
    prev_pad = jnp.pad(prev, ((0, 0), (HALO - (CONV_KERNEL - 1), 0), (0, 0)))
    c = conformer_conv(pad_q(g3), prev_pad, wts["dw"], wts["dw_bias"], wts["ln_g"], wts["ln_b"], QPAD)
    y_sample = _layer_back(xs, a, c[:, :ts].reshape(ns, cw), wts, ns).reshape(bs, ts, d_model)
    k_sample = kf.reshape(1, bs, ts, N_KV_HEADS, HEAD_DIM)
    v_sample = vf.reshape(1, bs, ts, N_KV_HEADS, HEAD_DIM)
    idx_k_sample = kif.reshape(1, bs, ts, IDX_DIM)
    conv_sample = jnp.concatenate([prev, g3], axis=1)[:, ts:][None]

    return (y_prompt, y_sample, k_prompt, v_prompt, idx_k_prompt, conv_prompt,
            k_sample, v_sample, idx_k_sample, conv_sample)
```

```python
import functools

import numpy as np
import jax
import jax.numpy as jnp
from jax import lax
from jax.experimental import pallas as pl
from jax.experimental.pallas import tpu as pltpu

N_HEADS = 8
HEAD_DIM = 128
N_KV_HEADS = 2
GROUP = N_HEADS // N_KV_HEADS
IDX_HEADS = 16
IDX_DIM = 64
TOPK_MAX = 256
ROPE_THETA = 10000.0
CONV_KERNEL = 31
EPS = 1e-6
PAGE = 128

LANES = 128
SUBLANES = 8
HALO = 32
VMEM_LIMIT = 56 * 1024 * 1024
INT_MIN = np.int32(-2 ** 31)
NEG_BIG = -1e30
LOG2E = 1.4426950408889634

BF16 = jnp.bfloat16
F32 = jnp.float32
I32 = jnp.int32


def _params(*sem):
    return pltpu.CompilerParams(dimension_semantics=sem, vmem_limit_bytes=VMEM_LIMIT)


def _rank_to_float(rank):
    key = rank ^ INT_MIN
    return pltpu.bitcast(key ^ ((key >> 31) & np.int32(0x7FFFFFFF)), F32)


def _kth_largest(count_ge, k):
    def body(it, acc):
        bit = lax.shift_left(np.int32(1), np.int32(31) - it)
        cand = acc | bit
        return jnp.where(count_ge(_rank_to_float(cand)) >= k, cand, acc)
    acc = lax.fori_loop(0, 32, body, jnp.zeros_like(k))
    return _rank_to_float(acc)


def _rms_mm_kernel(x_ref, g_ref, *refs, n_w, combine):
    w_refs, o_ref, n_ref = refs[:n_w], refs[n_w], refs[n_w + 1]

    @pl.when(pl.program_id(1) == 0)
    def _():
        x = x_ref[...]
        y = x * lax.rsqrt(jnp.mean(x * x, axis=-1, keepdims=True) + EPS)
        n_ref[...] = (y * g_ref[...]).astype(BF16)

    n = n_ref[...]
    outs = [jnp.dot(n, w[...], preferred_element_type=F32) for w in w_refs]
    o_ref[...] = combine(*outs).astype(o_ref.dtype)


def rms_mm(x, g, ws, combine, out_dtype, tm, tn):
    n_tok, d = x.shape
    f = ws[0].shape[1]
    kern = functools.partial(_rms_mm_kernel, n_w=len(ws), combine=combine)
    return pl.pallas_call(
        kern,
        grid=(n_tok // tm, f // tn),
        in_specs=[pl.BlockSpec((tm, d), lambda i, j: (i, 0)),
                  pl.BlockSpec((1, d), lambda i, j: (0, 0))]
                 + [pl.BlockSpec((d, tn), lambda i, j: (0, j)) for _ in ws],
        out_specs=pl.BlockSpec((tm, tn), lambda i, j: (i, j)),
        out_shape=jax.ShapeDtypeStruct((n_tok, f), out_dtype),
        scratch_shapes=[pltpu.VMEM((tm, d), BF16)],
        compiler_params=_params("parallel", "arbitrary"),
        name="rms_mm",
    )(x, g, *ws)


def _dual_mm_kernel(n_ref, w1_ref, w2_ref, o_ref):
    n = n_ref[...]
    a = jnp.dot(n, w1_ref[...], preferred_element_type=F32)
    b = jnp.dot(n, w2_ref[...], preferred_element_type=F32)
    o_ref[...] = (a * jax.nn.sigmoid(a) * b).astype(o_ref.dtype)


def ffn_up(n, w_gate, w_up, tm, tn):
    n_tok, d = n.shape
    f = w_gate.shape[1]
    return pl.pallas_call(
        _dual_mm_kernel,
        grid=(n_tok // tm, f // tn),
        in_specs=[pl.BlockSpec((tm, d), lambda i, j: (i, 0)),
                  pl.BlockSpec((d, tn), lambda i, j: (0, j)),
                  pl.BlockSpec((d, tn), lambda i, j: (0, j))],
        out_specs=pl.BlockSpec((tm, tn), lambda i, j: (i, j)),
        out_shape=jax.ShapeDtypeStruct((n_tok, f), BF16),
        compiler_params=_params("parallel", "arbitrary"),
        name="ffn_up",
    )(n, w_gate, w_up)


def _out_proj_kernel(a_ref, c_ref, wa_ref, wc_ref, x_ref, g_ref, h_ref, n2_ref):
    h = x_ref[...]
    h = h + jnp.dot(a_ref[...], wa_ref[...], preferred_element_type=F32)
    h = h + jnp.dot(c_ref[...], wc_ref[...], preferred_element_type=F32)
    h_ref[...] = h
    y = h * lax.rsqrt(jnp.mean(h * h, axis=-1, keepdims=True) + EPS)
    n2_ref[...] = (y * g_ref[...]).astype(BF16)


def out_proj(a, c, wa, wc, x, g, tm):
    n_tok, d = x.shape
    ka, kc = a.shape[1], c.shape[1]
    return pl.pallas_call(
        _out_proj_kernel,
        grid=(n_tok // tm,),
        in_specs=[pl.BlockSpec((tm, ka), lambda i: (i, 0)),
                  pl.BlockSpec((tm, kc), lambda i: (i, 0)),
                  pl.BlockSpec((ka, d), lambda i: (0, 0)),
                  pl.BlockSpec((kc, d), lambda i: (0, 0)),
                  pl.BlockSpec((tm, d), lambda i: (i, 0)),
                  pl.BlockSpec((1, d), lambda i: (0, 0))],
        out_specs=[pl.BlockSpec((tm, d), lambda i: (i, 0)),
                   pl.BlockSpec((tm, d), lambda i: (i, 0))],
        out_shape=[jax.ShapeDtypeStruct((n_tok, d), F32),
                   jax.ShapeDtypeStruct((n_tok, d), BF16)],
        compiler_params=_params("parallel"),
        name="out_proj",
    )(a, c, wa, wc, x, g)


def _down_kernel(a_ref, w_ref, h_ref, g_ref, y_ref, cols_ref):
    j = pl.program_id(1)
    cols_ref[j] = h_ref[...] + jnp.dot(a_ref[...], w_ref[...], preferred_element_type=F32)

    @pl.when(j == pl.num_programs(1) - 1)
    def _():
        h = jnp.concatenate([cols_ref[c] for c in range(cols_ref.shape[0])], axis=1)
        y = h * lax.rsqrt(jnp.mean(h * h, axis=-1, keepdims=True) + EPS)
        y_ref[...] = y * g_ref[...]


def ffn_down(act, w_down, h, g, tm, tn):
    n_tok, f = act.shape
    d = w_down.shape[1]
    return pl.pallas_call(
        _down_kernel,
        grid=(n_tok // tm, d // tn),
        in_specs=[pl.BlockSpec((tm, f), lambda i, j: (i, 0)),
                  pl.BlockSpec((f, tn), lambda i, j: (0, j)),
                  pl.BlockSpec((tm, tn), lambda i, j: (i, j)),
                  pl.BlockSpec((1, d), lambda i, j: (0, 0))],
        out_specs=pl.BlockSpec((tm, d), lambda i, j: (i, 0)),
        out_shape=jax.ShapeDtypeStruct((n_tok, d), F32),
        scratch_shapes=[pltpu.VMEM((d // tn, tm, tn), F32)],
        compiler_params=_params("parallel", "arbitrary"),
        name="ffn_down",
    )(act, w_down, h, g)


def _rope128(x, cos, sin):
    return x * cos + pltpu.roll(x, 64, 1) * sin


def _rope64(x, cos, sin, lane):
    partner = jnp.where((lane & 63) < 32, pltpu.roll(x, 96, 1), pltpu.roll(x, 32, 1))
    return x * cos + partner * sin


def _attn_front_kernel(x_ref, g_ref, w_ref, c128_ref, s128_ref, c64_ref, s64_ref,
                       q_ref, kf_ref, kb_ref, vf_ref, vb_ref, qi_ref, kif_ref, kib_ref, wi_ref):
    x = x_ref[...]
    y = x * lax.rsqrt(jnp.mean(x * x, axis=-1, keepdims=True) + EPS)
    n = (y * g_ref[...]).astype(BF16)
    p = jnp.dot(n, w_ref[...], preferred_element_type=F32)
    c128, s128 = c128_ref[...], s128_ref[...]
    c64, s64 = c64_ref[...], s64_ref[...]
    tm = x_ref.shape[0]
    lane = lax.broadcasted_iota(I32, (tm, LANES), 1)
    for h in range(N_HEADS):
        q_ref[:, h * 128:(h + 1) * 128] = _rope128(p[:, h * 128:(h + 1) * 128], c128, s128).astype(BF16)
    for h in range(N_KV_HEADS):
        r = _rope128(p[:, 1024 + h * 128:1024 + (h + 1) * 128], c128, s128)
        kf_ref[:, h * 128:(h + 1) * 128] = r
        kb_ref[:, h * 128:(h + 1) * 128] = r.astype(BF16)
    v = p[:, 1280:1536]
    vf_ref[...] = v
    vb_ref[...] = v.astype(BF16)
    for hp in range(IDX_HEADS // 2):
        r = _rope64(p[:, 1536 + hp * 128:1536 + (hp + 1) * 128], c64, s64, lane)
        qi_ref[2 * hp] = r[:, :64].astype(BF16)
        qi_ref[2 * hp + 1] = r[:, 64:].astype(BF16)
    x = p[:, 2560:2688]
    r = _rope64(x, c64, s64, lane)
    kif_ref[...] = r[:, :64]
    kib_ref[...] = r[:, :64].astype(BF16)
    wi_ref[...] = x[:, 64:64 + IDX_HEADS] * (IDX_HEADS ** -0.5 * IDX_DIM ** -0.5)


def attn_front(x, g, w, tabs, n_pos_tiles, tm):
    n_tok, d = x.shape
    tab_spec = pl.BlockSpec((tm, LANES), lambda i: (i % n_pos_tiles, 0))
    row = lambda w: pl.BlockSpec((tm, w), lambda i: (i, 0))
    sds = jax.ShapeDtypeStruct
    return pl.pallas_call(
        _attn_front_kernel,
        grid=(n_tok // tm,),
        in_specs=[pl.BlockSpec((tm, d), lambda i: (i, 0)),
                  pl.BlockSpec((1, d), lambda i: (0, 0)),
                  pl.BlockSpec(w.shape, lambda i: (0, 0))] + [tab_spec] * 4,
        out_specs=[row(1024), row(256), row(256), row(256), row(256),
                   pl.BlockSpec((IDX_HEADS, tm, IDX_DIM), lambda i: (0, i, 0)),
                   row(IDX_DIM), row(IDX_DIM), row(IDX_HEADS)],
        out_shape=[sds((n_tok, 1024), BF16), sds((n_tok, 256), F32), sds((n_tok, 256), BF16),
                   sds((n_tok, 256), F32), sds((n_tok, 256), BF16),
                   sds((IDX_HEADS, n_tok, IDX_DIM), BF16),
                   sds((n_tok, IDX_DIM), F32), sds((n_tok, IDX_DIM), BF16),
                   sds((n_tok, IDX_HEADS), F32)],
        compiler_params=_params("parallel"),
        name="attn_front",
    )(x, g, w, *tabs)


def rope_tables(pos):
    pos = pos.astype(F32)[:, None]

    def tab(half):
        freqs = ROPE_THETA ** (-jnp.arange(half, dtype=F32) / half)
        ang = pos * freqs[None, :]
        c, s = jnp.cos(ang), jnp.sin(ang)
        reps = LANES // (2 * half)
        return jnp.tile(jnp.concatenate([c, c], -1), (1, reps)), jnp.tile(jnp.concatenate([-s, s], -1), (1, reps))

    c128, s128 = tab(HEAD_DIM // 2)
    c64, s64 = tab(IDX_DIM // 2)
    return c128, s128, c64, s64


def _prompt_attn_kernel(q_ref, k_ref, v_ref, qi_ref, ki_ref, wit_ref, o_ref,
                        sc_ref, qs_ref, m_ref, l_ref, acc_ref, *, topk, kc, cpb):
    i = pl.program_id(1)
    tq = q_ref.shape[1]
    nchunk = ((i + 1) * tq + kc - 1) // kc
    gq = GROUP * tq

    def chunk_rows(c):
        return pl.ds(pl.multiple_of(c * kc, kc), kc)

    qi_all = qi_ref[...].reshape(IDX_HEADS * tq, IDX_DIM)
    qpos = i * tq + lax.broadcasted_iota(I32, (kc, tq), 1)

    def score_chunk(c, carry):
        d = lax.dot_general(ki_ref[0, chunk_rows(c), :], qi_all, (((1,), (1,)), ((), ())),
                            preferred_element_type=F32)
        tot = jnp.zeros((kc, tq), F32)
        for h in range(IDX_HEADS):
            tot = tot + wit_ref[0, h:h + 1, :] * jnp.maximum(d[:, h * tq:(h + 1) * tq], 0.0)
        kpos = c * kc + lax.broadcasted_iota(I32, (kc, tq), 0)
        sc_ref[chunk_rows(c), :] = jnp.where(kpos <= qpos, tot, -jnp.inf)
        return carry

    lax.fori_loop(0, nchunk, score_chunk, 0)

    kq = jnp.minimum(topk, i * tq + 1 + lax.broadcasted_iota(I32, (1, tq), 1))
    nacc = 4 * 8

    @pl.when(nchunk % cpb != 0)
    def _():
        sc_ref[chunk_rows(nchunk), :] = jnp.full((kc, tq), -jnp.inf, F32)

    def count_ge(cand):
        def blk(c, acc):
            rows = pl.ds(pl.multiple_of(c * (cpb * kc), cpb * kc), cpb * kc)
            hit = jnp.where(sc_ref[rows, :] >= cand, 1, 0).astype(I32)
            return acc + jnp.sum(hit.reshape(cpb * kc // nacc, nacc, tq), axis=0)
        acc = lax.fori_loop(0, (nchunk + cpb - 1) // cpb, blk, jnp.zeros((nacc, tq), I32))
        return jnp.sum(acc, axis=0, keepdims=True)

    thr = _kth_largest(count_ge, kq)

    for h in range(N_HEADS):
        g, hh = divmod(h, GROUP)
        qs_ref[g, hh * tq:(hh + 1) * tq, :] = q_ref[0, :, h * HEAD_DIM:(h + 1) * HEAD_DIM]
    m_ref[...] = jnp.full(m_ref.shape, NEG_BIG, F32)
    l_ref[...] = jnp.zeros(l_ref.shape, F32)
    acc_ref[...] = jnp.zeros(acc_ref.shape, F32)
    c2 = HEAD_DIM ** -0.5 * LOG2E

    def attn_chunk(c, carry):
        rows = chunk_rows(c)
        bias_t = jnp.where(sc_ref[rows, :] >= thr, 0.0, NEG_BIG).astype(F32)
        bias = jnp.transpose(bias_t)
        for g in range(N_KV_HEADS):
            kg = k_ref[0, rows, g * HEAD_DIM:(g + 1) * HEAD_DIM]
            vg = v_ref[0, rows, g * HEAD_DIM:(g + 1) * HEAD_DIM]
            s = lax.dot_general(qs_ref[g], kg, (((1,), (1,)), ((), ())), preferred_element_type=F32)
            s = (s.reshape(GROUP, tq, kc) + bias[None]).reshape(gq, kc)
            m_old = m_ref[g]
            m_new = jnp.maximum(m_old, jnp.max(s, axis=1, keepdims=True))
            p = jnp.exp2((s - jnp.tile(m_new, (1, kc // LANES))) * c2)
            alpha = jnp.exp2((m_old - m_new) * c2)
            l_ref[g] = alpha * l_ref[g] + jnp.sum(p, axis=1, keepdims=True)
            acc_ref[g] = alpha * acc_ref[g] + jnp.dot(p.astype(BF16), vg, preferred_element_type=F32)
            m_ref[g] = m_new
        return carry

    lax.fori_loop(0, nchunk, attn_chunk, 0)

    for h in range(N_HEADS):
        g, hh = divmod(h, GROUP)
        rs = slice(hh * tq, (hh + 1) * tq)
        o_ref[0, :, h * HEAD_DIM:(h + 1) * HEAD_DIM] = (acc_ref[g, rs, :] / l_ref[g, rs, :]).astype(o_ref.dtype)


def prompt_attention(q, k, v, qi, ki, wit, batch, seq, tq=LANES):
    nt = seq // tq
    kc = min(seq, 512)
    q3 = q.reshape(batch, seq, N_HEADS * HEAD_DIM)
    k3 = k.reshape(batch, seq, N_KV_HEADS * HEAD_DIM)
    v3 = v.reshape(batch, seq, N_KV_HEADS * HEAD_DIM)
    ki3 = ki.reshape(batch, seq, IDX_DIM)
    out = pl.pallas_call(
        functools.partial(_prompt_attn_kernel, topk=min(TOPK_MAX, seq // 4), kc=kc,
                          cpb=2 if seq % (2 * kc) == 0 else 1),
        grid=(batch, nt),
        in_specs=[pl.BlockSpec((1, tq, N_HEADS * HEAD_DIM), lambda b, i: (b, i, 0)),
                  pl.BlockSpec((1, seq, N_KV_HEADS * HEAD_DIM), lambda b, i: (b, 0, 0)),
                  pl.BlockSpec((1, seq, N_KV_HEADS * HEAD_DIM), lambda b, i: (b, 0, 0)),
                  pl.BlockSpec((IDX_HEADS, tq, IDX_DIM), lambda b, i: (0, b * nt + i, 0)),
                  pl.BlockSpec((1, seq, IDX_DIM), lambda b, i: (b, 0, 0)),
                  pl.BlockSpec((1, IDX_HEADS, tq), lambda b, i: (b, 0, i))],
        out_specs=pl.BlockSpec((1, tq, N_HEADS * HEAD_DIM), lambda b, i: (b, i, 0)),
        out_shape=jax.ShapeDtypeStruct((batch, seq, N_HEADS * HEAD_DIM), BF16),
        scratch_shapes=[pltpu.VMEM((seq, tq), F32),
                        pltpu.VMEM((N_KV_HEADS, GROUP * tq, HEAD_DIM), BF16),
                        pltpu.VMEM((N_KV_HEADS, GROUP * tq, LANES), F32),
                        pltpu.VMEM((N_KV_HEADS, GROUP * tq, LANES), F32),
                        pltpu.VMEM((N_KV_HEADS, GROUP * tq, HEAD_DIM), F32)],
        compiler_params=_params("parallel", "arbitrary"),
        name="prompt_attn",
    )(q3, k3, v3, qi, ki3, wit)
    return out.reshape(batch * seq, N_HEADS * HEAD_DIM)


def _conv_kernel(g_ref, gh_ref, prev_ref, dw_ref, b_ref, lg_ref, lb_ref, o_ref, buf_ref, sh_ref, y_ref):
    i = pl.program_id(1)
    tt, ch = g_ref.shape[1], g_ref.shape[2]
    buf_ref[HALO:, :] = g_ref[0]

    @pl.when(i == 0)
    def _():
        buf_ref[:HALO, :] = prev_ref[0]

    @pl.when(i > 0)
    def _():
        buf_ref[:HALO, :] = gh_ref[0]

    off = HALO - (CONV_KERNEL - 1)
    span = sh_ref.shape[1]
    for r in range(1, SUBLANES):
        sh_ref[r - 1] = buf_ref[r:r + span, :]
    rb = min(tt, 64)
    for cb in range(ch // LANES):
        lanes = slice(cb * LANES, (cb + 1) * LANES)
        w = [dw_ref[j:j + 1, lanes] for j in range(CONV_KERNEL)]
        for r0 in range(0, tt, rb):
            acc = jnp.zeros((rb, LANES), F32)
            for j in range(CONV_KERNEL):
                r = (off + j) % SUBLANES
                q = off + j - r + r0
                src = buf_ref if r == 0 else sh_ref.at[r - 1]
                acc = acc + src[q:q + rb, lanes] * w[j]
            y_ref[r0:r0 + rb, lanes] = acc
    y = y_ref[...] + b_ref[...]
    mu = jnp.mean(y, axis=-1, keepdims=True)
    yc = y - mu
    z = yc * lax.rsqrt(jnp.mean(yc * yc, axis=-1, keepdims=True) + EPS)
    z = z * lg_ref[...] + lb_ref[...]
    o_ref[0] = (z * jax.nn.sigmoid(z)).astype(o_ref.dtype)


def conformer_conv(g, prev, dw, bias, ln_g, ln_b, tt):
    batch, seq, ch = g.shape
    per = tt // HALO
    halo_src = g if seq > tt else prev
    return pl.pallas_call(
        _conv_kernel,
        grid=(batch, seq // tt),
        in_specs=[pl.BlockSpec((1, tt, ch), lambda b, i: (b, i, 0)),
                  pl.BlockSpec((1, HALO, ch), lambda b, i: (b, jnp.maximum(i * per - 1, 0), 0)),
                  pl.BlockSpec((1, HALO, ch), lambda b, i: (b, 0, 0)),
                  pl.BlockSpec((HALO, ch), lambda b, i: (0, 0)),
                  pl.BlockSpec((1, ch), lambda b, i: (0, 0)),
                  pl.BlockSpec((1, ch), lambda b, i: (0, 0)),
                  pl.BlockSpec((1, ch), lambda b, i: (0, 0))],
        out_specs=pl.BlockSpec((1, tt, ch), lambda b, i: (b, i, 0)),
        out_shape=jax.ShapeDtypeStruct((batch, seq, ch), BF16),
        scratch_shapes=[pltpu.VMEM((HALO + tt, ch), F32),
                        pltpu.VMEM((SUBLANES - 1, HALO + tt - SUBLANES, ch), F32),
                        pltpu.VMEM((tt, ch), F32)],
        compiler_params=_params("parallel", "arbitrary"),
        name="conformer_conv",
    )(g, halo_src, prev, dw, bias, ln_g, ln_b)


PPS = 16
PPA = 16
QPAD = 8


def _sample_scores_kernel(pt_ref, qi_ref, w_ref, kin_ref, *refs):
    pages, (i_ref, inew_ref) = refs[:PPS], refs[PPS:]
    s = pl.program_id(1)
    qi = qi_ref[0]
    w = w_ref[0]

    def scores(keys_t):
        d = jnp.dot(qi, keys_t, preferred_element_type=F32)
        e = w * jnp.maximum(d, 0.0)
        return jnp.sum(e.reshape(IDX_HEADS, QPAD, e.shape[1]), axis=0)

    i_ref[0] = scores(jnp.concatenate([p[0].astype(BF16) for p in pages], axis=1))

    @pl.when(s == 0)
    def _():
        sn = scores(kin_ref[0])
        qrow = lax.broadcasted_iota(I32, sn.shape, 0)
        kcol = lax.broadcasted_iota(I32, sn.shape, 1)
        inew_ref[0] = jnp.where(kcol <= qrow, sn, -jnp.inf)


def sample_scores(page_table, qi, w, ki_new_t, cache_idx_kt):
    bs, n_pages = page_table.shape
    grid_spec = pltpu.PrefetchScalarGridSpec(
        num_scalar_prefetch=1,
        grid=(bs, n_pages // PPS),
        in_specs=[pl.BlockSpec((1, IDX_HEADS * QPAD, IDX_DIM), lambda b, s, pt: (b, 0, 0)),
                  pl.BlockSpec((1, IDX_HEADS * QPAD, 1), lambda b, s, pt: (b, 0, 0)),
                  pl.BlockSpec((1, IDX_DIM, PAGE), lambda b, s, pt: (b, 0, 0))]
                 + [pl.BlockSpec((1, IDX_DIM, PAGE),
                                 functools.partial(lambda b, s, pt, r: (pt[b * n_pages + s * PPS + r], 0, 0), r=r))
                    for r in range(PPS)],
        out_specs=[pl.BlockSpec((1, QPAD, PPS * PAGE), lambda b, s, pt: (b, 0, s)),
                   pl.BlockSpec((1, QPAD, PAGE), lambda b, s, pt: (b, 0, 0))],
    )
    return pl.pallas_call(
        _sample_scores_kernel,
        grid_spec=grid_spec,
        out_shape=[jax.ShapeDtypeStruct((bs, QPAD, n_pages * PAGE), F32),
                   jax.ShapeDtypeStruct((bs, QPAD, PAGE), F32)],
        compiler_params=_params("parallel", "arbitrary"),
        name="sample_scores",
    )(page_table.reshape(-1), qi, w, ki_new_t, *([cache_idx_kt] * PPS))


def _sample_thr_kernel(i_ref, thr_ref, *, k):
    rows, width = i_ref.shape
    nblk = width // LANES

    def count_ge(cand):
        def blk(c, acc):
            s = i_ref[:, pl.ds(pl.multiple_of(c * LANES, LANES), LANES)]
            return acc + jnp.where(s >= cand, 1, 0).astype(I32)
        acc = lax.fori_loop(0, nblk, blk, jnp.zeros((rows, LANES), I32))
        return jnp.sum(acc, axis=1, keepdims=True)

    thr = _kth_largest(count_ge, jnp.full((rows, 1), k, I32))
    thr_ref[...] = jnp.broadcast_to(thr, thr_ref.shape)


def sample_threshold(scores, k):
    rows, width = scores.shape
    return pl.pallas_call(
        functools.partial(_sample_thr_kernel, k=k),
        out_shape=jax.ShapeDtypeStruct((rows, LANES), F32),
        compiler_params=pltpu.CompilerParams(vmem_limit_bytes=VMEM_LIMIT),
        name="sample_threshold",
    )(scores)


def _sample_attn_kernel(pt_ref, q_ref, i_ref, inew_ref, thr_ref, kn_ref, vn_ref, *refs):
    kpages, vpages = refs[:PPA], refs[PPA:2 * PPA]
    o_ref, m_ref, l_ref, acc_ref = refs[2 * PPA:]
    s = pl.program_id(1)
    c2 = HEAD_DIM ** -0.5 * LOG2E
    rows_g = GROUP * QPAD
    thr = thr_ref[0]

    @pl.when(s == 0)
    def _():
        m_ref[...] = jnp.full(m_ref.shape, NEG_BIG, F32)
        l_ref[...] = jnp.zeros(l_ref.shape, F32)
        acc_ref[...] = jnp.zeros(acc_ref.shape, F32)

    def update(scores_blk, kv_of_group):
        nk = scores_blk.shape[1]
        bias = jnp.where(scores_blk >= jnp.tile(thr, (1, nk // LANES)), 0.0, NEG_BIG).astype(F32)
        for g in range(N_KV_HEADS):
            rs = slice(g * rows_g, (g + 1) * rows_g)
            kg, vg = kv_of_group(g)
            sc = lax.dot_general(q_ref[0, rs, :], kg, (((1,), (1,)), ((), ())), preferred_element_type=F32)
            sc = (sc.reshape(GROUP, QPAD, nk) + bias[None]).reshape(rows_g, nk)
            m_old = m_ref[rs, :]
            m_new = jnp.maximum(m_old, jnp.max(sc, axis=1, keepdims=True))
            p = jnp.exp2((sc - jnp.tile(m_new, (1, nk // LANES))) * c2)
            alpha = jnp.exp2((m_old - m_new) * c2)
            l_ref[rs, :] = alpha * l_ref[rs, :] + jnp.sum(p, axis=1, keepdims=True)
            acc_ref[rs, :] = alpha * acc_ref[rs, :] + jnp.dot(p.astype(BF16), vg, preferred_element_type=F32)
            m_ref[rs, :] = m_new

    def paged_kv(g):
        rows = pl.ds(g, PAGE, stride=N_KV_HEADS)
        kg = jnp.concatenate([p[rows, :].astype(BF16) for p in kpages], axis=0)
        vg = jnp.concatenate([p[rows, :].astype(BF16) for p in vpages], axis=0)
        return kg, vg

    update(i_ref[0], paged_kv)

    @pl.when(s == pl.num_programs(1) - 1)
    def _():
        update(inew_ref[0], lambda g: (kn_ref[0, :, g * HEAD_DIM:(g + 1) * HEAD_DIM],
                                       vn_ref[0, :, g * HEAD_DIM:(g + 1) * HEAD_DIM]))
        o_ref[0] = (acc_ref[...] / l_ref[...]).astype(o_ref.dtype)


def sample_attention(page_table, q, scores, scores_new, thr, k_new, v_new, cache_k_rows, cache_v_rows):
    bs, n_pages = page_table.shape
    kvw = N_KV_HEADS * HEAD_DIM
    page_rows = PAGE * N_KV_HEADS
    page_spec = lambda r: pl.BlockSpec(
        (page_rows, HEAD_DIM), functools.partial(lambda b, s, pt, r: (pt[b * n_pages + s * PPA + r], 0), r=r))
    fixed = lambda shape: pl.BlockSpec(shape, lambda b, s, pt: (b, 0, 0))
    grid_spec = pltpu.PrefetchScalarGridSpec(
        num_scalar_prefetch=1,
        grid=(bs, n_pages // PPA),
        in_specs=[fixed((1, N_HEADS * QPAD, HEAD_DIM)),
                  pl.BlockSpec((1, QPAD, PPA * PAGE), lambda b, s, pt: (b, 0, s)),
                  fixed((1, QPAD, PAGE)), fixed((1, QPAD, LANES)),
                  fixed((1, PAGE, kvw)), fixed((1, PAGE, kvw))]
                 + [page_spec(r) for r in range(PPA)] + [page_spec(r) for r in range(PPA)],
        out_specs=fixed((1, N_HEADS * QPAD, HEAD_DIM)),
        scratch_shapes=[pltpu.VMEM((N_HEADS * QPAD, LANES), F32),
                        pltpu.VMEM((N_HEADS * QPAD, LANES), F32),
                        pltpu.VMEM((N_HEADS * QPAD, HEAD_DIM), F32)],
    )
    return pl.pallas_call(
        _sample_attn_kernel,
        grid_spec=grid_spec,
        out_shape=jax.ShapeDtypeStruct((bs, N_HEADS * QPAD, HEAD_DIM), BF16),
        compiler_params=_params("parallel", "arbitrary"),
        name="sample_attn",
    )(page_table.reshape(-1), q, scores, scores_new, thr, k_new, v_new,
      *([cache_k_rows] * PPA), *([cache_v_rows] * PPA))


def _glu(a, gate):
    return a * jax.nn.sigmoid(gate)


def _token_tile(n_tok, pref):
    return pref if n_tok % pref == 0 else n_tok


def _layer_front(x2, pos_tabs, n_pos_tiles, wts, tm):
    tf = min(tm, 256)
    prep = attn_front(x2, wts["norm_attn"], wts["w_a"], pos_tabs, n_pos_tiles * (tm // tf), tf)
    g = rms_mm(x2, wts["norm_attn"], [wts["w_u1"], wts["w_u2"]], _glu, F32, tm, wts["w_u1"].shape[1])
    return prep, g


def _layer_back(x2, a, c, wts, tm):
    h, n2 = out_proj(a, c, wts["w_oa"], wts["w_oc"], x2, wts["norm_ffn"], min(tm, 256))
    act = ffn_up(n2, wts["w_gate"], wts["w_up"], tm, 512)
    return ffn_down(act, wts["w_down"], h, wts["norm_final"], tm, 512)


def _prep_weights(l, norm_attn, w_in, conv_dw, conv_dw_bias, conv_ln_g, conv_ln_b, w_out, norm_ffn,
                  w_gate, w_up, w_down, norm_final):
    w = w_in[l]
    aw = N_HEADS * HEAD_DIM
    kvw = N_KV_HEADS * HEAD_DIM
    o_small = aw + 2 * kvw + IDX_HEADS * IDX_DIM
    o_u = o_small + IDX_DIM + IDX_HEADS
    cw = (w.shape[1] - o_u) // 2
    pad = jnp.zeros((w.shape[0], LANES - IDX_DIM - IDX_HEADS), w.dtype)
    w_a = jnp.concatenate([w[:, :o_u], pad], axis=1).astype(BF16)
    dw = jnp.concatenate([conv_dw[l], jnp.zeros((HALO - CONV_KERNEL, cw), F32)], axis=0)
    return {
        "norm_attn": norm_attn[l][None, :], "w_a": w_a,
        "w_u1": w[:, o_u:o_u + cw].astype(BF16), "w_u2": w[:, o_u + cw:].astype(BF16),
        "dw": dw, "dw_bias": conv_dw_bias[l][None, :], "ln_g": conv_ln_g[l][None, :], "ln_b": conv_ln_b[l][None, :],
        "w_oa": w_out[l][:aw].astype(BF16), "w_oc": w_out[l][aw:].astype(BF16),
        "norm_ffn": norm_ffn[l][None, :],
        "w_gate": w_gate[l].astype(BF16), "w_up": w_up[l].astype(BF16), "w_down": w_down[l].astype(BF16),
        "norm_final": norm_final[None, :],
    }


def kernel(x_prompt, x_sample, cache_k, cache_v, cache_idx_k, state_conv, page_table, norm_attn, w_in, conv_dw, conv_dw_bias, conv_ln_g, conv_ln_b, w_out, norm_ffn, w_gate, w_up, w_down, norm_final):
    batch, seq, d_model = x_prompt.shape
    bs, ts = x_sample.shape[:2]
    past = page_table.shape[1] * PAGE
    cw = conv_dw.shape[2]
    assert norm_attn.shape[0] == 1, "single-layer step"
    wts = _prep_weights(0, norm_attn, w_in, conv_dw, conv_dw_bias, conv_ln_g, conv_ln_b, w_out, norm_ffn,
                        w_gate, w_up, w_down, norm_final)

    tm = _token_tile(batch * seq, 512)
    xp = x_prompt.reshape(batch * seq, d_model)
    tabs_p = rope_tables(jnp.arange(seq, dtype=jnp.int32))
    (q, kf, kb, vf, vb, qi, kif, kib, wi), g = _layer_front(xp, tabs_p, seq // tm, wts, tm)
    wit = wi.reshape(batch, seq, IDX_HEADS).transpose(0, 2, 1)
    a = prompt_attention(q, kb, vb, qi, kib, wit, batch, seq)
    g3 = g.reshape(batch, seq, cw)
    c = conformer_conv(g3, jnp.zeros((batch, HALO, cw), F32), wts["dw"], wts["dw_bias"], wts["ln_g"], wts["ln_b"],
                       min(seq, 256))
    y_prompt = _layer_back(xp, a, c.reshape(batch * seq, cw), wts, tm).reshape(batch, seq, d_model)
    k_prompt = kf.reshape(1, batch, seq, N_KV_HEADS, HEAD_DIM)
    v_prompt = vf.reshape(1, batch, seq, N_KV_HEADS, HEAD_DIM)
    idx_k_prompt = kif.reshape(1, batch, seq, IDX_DIM)
    conv_prompt = g3[:, seq - (CONV_KERNEL - 1):][None]

    ns = bs * ts
    xs = x_sample.reshape(ns, d_model)
    tabs_s = tuple(jnp.tile(t, (bs, 1)) for t in rope_tables(past + jnp.arange(ts, dtype=jnp.int32)))
    (q, kf, kb, vf, vb, qi, kif, kib, wi), g = _layer_front(xs, tabs_s, 1, wts, ns)

    def pad_q(t):
        return jnp.pad(t, ((0, 0), (0, QPAD - ts)) + ((0, 0),) * (t.ndim - 2))

    def pad_keys(t):
        return jnp.pad(t, ((0, 0), (0, PAGE - ts), (0, 0)))

    qi_s = pad_q(qi.reshape(IDX_HEADS, bs, ts, IDX_DIM).transpose(1, 2, 0, 3))
    qi_s = qi_s.transpose(0, 2, 1, 3).reshape(bs, IDX_HEADS * QPAD, IDX_DIM)
    w_s = pad_q(wi.reshape(bs, ts, IDX_HEADS)).transpose(0, 2, 1).reshape(bs, IDX_HEADS * QPAD, 1)
    ki_new_t = jnp.swapaxes(pad_keys(kib.reshape(bs, ts, IDX_DIM)), 1, 2)
    sc_past, sc_new = sample_scores(page_table, qi_s, w_s, ki_new_t, jnp.swapaxes(cache_idx_k[0], 1, 2))
    sc_all = jnp.concatenate([sc_past[:, :ts], sc_new[:, :ts]], axis=-1).reshape(ns, past + PAGE)
    thr = sample_threshold(sc_all, min(TOPK_MAX, (past + ts) // 4))
    thr = pad_q(thr.reshape(bs, ts, LANES))
    q_s = pad_q(q.reshape(bs, ts, N_HEADS, HEAD_DIM)).transpose(0, 2, 1, 3).reshape(bs, N_HEADS * QPAD, HEAD_DIM)
    kvw = N_KV_HEADS * HEAD_DIM
    o = sample_attention(page_table, q_s, sc_past, sc_new, thr,
                         pad_keys(kb.reshape(bs, ts, kvw)), pad_keys(vb.reshape(bs, ts, kvw)),
                         cache_k[0].reshape(-1, HEAD_DIM), cache_v[0].reshape(-1, HEAD_DIM))
    a = o.reshape(bs, N_HEADS, QPAD, HEAD_DIM)[:, :, :ts].transpose(0, 2, 1, 3).reshape(ns, N_HEADS * HEAD_DIM)

    g3 = g.reshape(bs, ts, cw)
    prev = state_conv[0]
    prev_pad = jnp.pad(prev, ((0, 0), (HALO - (CONV_KERNEL - 1), 0), (0, 0)))
    c = conformer_conv(pad_q(g3), prev_pad, wts["dw"], wts["dw_bias"], wts["ln_g"], wts["ln_b"], QPAD)
    y_sample = _layer_back(xs, a, c[:, :ts].reshape(ns, cw), wts, ns).reshape(bs, ts, d_model)
    k_sample = kf.reshape(1, bs, ts, N_KV_HEADS, HEAD_DIM)
    v_sample = vf.reshape(1, bs, ts, N_KV_HEADS, HEAD_DIM)
    idx_k_sample = kif.reshape(1, bs, ts, IDX_DIM)
    conv_sample = jnp.concatenate([prev, g3], axis=1)[:, ts:][None]

    return (y_prompt, y_sample, k_prompt, v_prompt, idx_k_prompt, conv_prompt,
            k_sample, v_sample, idx_k_sample, conv_sample)
```

```python
import functools

import numpy as np
import jax
import jax.numpy as jnp
from jax import lax
from jax.experimental import pallas as pl
from jax.experimental.pallas import tpu as pltpu

N_HEADS = 8
HEAD_DIM = 128
N_KV_HEADS = 2
GROUP = N_HEADS // N_KV_HEADS
IDX_HEADS = 16
IDX_DIM = 64
TOPK_MAX = 256
ROPE_THETA = 10000.0
CONV_KERNEL = 31
EPS = 1e-6
PAGE = 128

LANES = 128
SUBLANES = 8
HALO = 32
VMEM_LIMIT = 56 * 1024 * 1024
INT_MIN = np.int32(-2 ** 31)
NEG_BIG = -1e30
LOG2E = 1.4426950408889634

BF16 = jnp.bfloat16
F32 = jnp.float32
I32 = jnp.int32


def _params(*sem):
    return pltpu.CompilerParams(dimension_semantics=sem, vmem_limit_bytes=VMEM_LIMIT)


def _rank_to_float(rank):
    key = rank ^ INT_MIN
    return pltpu.bitcast(key ^ ((key >> 31) & np.int32(0x7FFFFFFF)), F32)


def _kth_largest(count_ge, k):
    def body(it, acc):
        bit = lax.shift_left(np.int32(1), np.int32(31) - it)
        cand = acc | bit
        return jnp.where(count_ge(_rank_to_float(cand)) >= k, cand, acc)
    acc = lax.fori_loop(0, 32, body, jnp.zeros_like(k))
    return _rank_to_float(acc)


def _rms_mm_kernel(x_ref, g_ref, *refs, n_w, combine):
    w_refs, o_ref, n_ref = refs[:n_w], refs[n_w], refs[n_w + 1]

    @pl.when(pl.program_id(1) == 0)
    def _():
        x = x_ref[...]
        y = x * lax.rsqrt(jnp.mean(x * x, axis=-1, keepdims=True) + EPS)
        n_ref[...] = (y * g_ref[...]).astype(BF16)

    n = n_ref[...]
    outs = [jnp.dot(n, w[...], preferred_element_type=F32) for w in w_refs]
    o_ref[...] = combine(*outs).astype(o_ref.dtype)


def rms_mm(x, g, ws, combine, out_dtype, tm, tn):
    n_tok, d = x.shape
    f = ws[0].shape[1]
    kern = functools.partial(_rms_mm_kernel, n_w=len(ws), combine=combine)
    return pl.pallas_call(
        kern,
        grid=(n_tok // tm, f // tn),
        in_specs=[pl.BlockSpec((tm, d), lambda i, j: (i, 0)),
                  pl.BlockSpec((1, d), lambda i, j: (0, 0))]
                 + [pl.BlockSpec((d, tn), lambda i, j: (0, j)) for _ in ws],
        out_specs=pl.BlockSpec((tm, tn), lambda i, j: (i, j)),
        out_shape=jax.ShapeDtypeStruct((n_tok, f), out_dtype),
        scratch_shapes=[pltpu.VMEM((tm, d), BF16)],
        compiler_params=_params("parallel", "arbitrary"),
        name="rms_mm",
    )(x, g, *ws)


def _dual_mm_kernel(n_ref, w1_ref, w2_ref, o_ref):
    n = n_ref[...]
    a = jnp.dot(n, w1_ref[...], preferred_element_type=F32)
    b = jnp.dot(n, w2_ref[...], preferred_element_type=F32)
    o_ref[...] = (a * jax.nn.sigmoid(a) * b).astype(o_ref.dtype)


def ffn_up(n, w_gate, w_up, tm, tn):
    n_tok, d = n.shape
    f = w_gate.shape[1]
    return pl.pallas_call(
        _dual_mm_kernel,
        grid=(n_tok // tm, f // tn),
        in_specs=[pl.BlockSpec((tm, d), lambda i, j: (i, 0)),
                  pl.BlockSpec((d, tn), lambda i, j: (0, j)),
                  pl.BlockSpec((d, tn), lambda i, j: (0, j))],
        out_specs=pl.BlockSpec((tm, tn), lambda i, j: (i, j)),
        out_shape=jax.ShapeDtypeStruct((n_tok, f), BF16),
        compiler_params=_params("parallel", "arbitrary"),
        name="ffn_up",
    )(n, w_gate, w_up)


def _out_proj_kernel(a_ref, c_ref, wa_ref, wc_ref, x_ref, g_ref, h_ref, n2_ref):
    h = x_ref[...]
    h = h + jnp.dot(a_ref[...], wa_ref[...], preferred_element_type=F32)
    h = h + jnp.dot(c_ref[...], wc_ref[...], preferred_element_type=F32)
    h_ref[...] = h
    y = h * lax.rsqrt(jnp.mean(h * h, axis=-1, keepdims=True) + EPS)
    n2_ref[...] = (y * g_ref[...]).astype(BF16)


def out_proj(a, c, wa, wc, x, g, tm):
    n_tok, d = x.shape
    ka, kc = a.shape[1], c.shape[1]
    return pl.pallas_call(
        _out_proj_kernel,
        grid=(n_tok // tm,),
        in_specs=[pl.BlockSpec((tm, ka), lambda i: (i, 0)),
                  pl.BlockSpec((tm, kc), lambda i: (i, 0)),
                  pl.BlockSpec((ka, d), lambda i: (0, 0)),
                  pl.BlockSpec((kc, d), lambda i: (0, 0)),
                  pl.BlockSpec((tm, d), lambda i: (i, 0)),
                  pl.BlockSpec((1, d), lambda i: (0, 0))],
        out_specs=[pl.BlockSpec((tm, d), lambda i: (i, 0)),
                   pl.BlockSpec((tm, d), lambda i: (i, 0))],
        out_shape=[jax.ShapeDtypeStruct((n_tok, d), F32),
                   jax.ShapeDtypeStruct((n_tok, d), BF16)],
        compiler_params=_params("parallel"),
        name="out_proj",
    )(a, c, wa, wc, x, g)


def _down_kernel(a_ref, w_ref, h_ref, g_ref, y_ref, cols_ref):
    j = pl.program_id(1)
    cols_ref[j] = h_ref[...] + jnp.dot(a_ref[...], w_ref[...], preferred_element_type=F32)

    @pl.when(j == pl.num_programs(1) - 1)
    def _():
        h = jnp.concatenate([cols_ref[c] for c in range(cols_ref.shape[0])], axis=1)
        y = h * lax.rsqrt(jnp.mean(h * h, axis=-1, keepdims=True) + EPS)
        y_ref[...] = y * g_ref[...]


def ffn_down(act, w_down, h, g, tm, tn):
    n_tok, f = act.shape
    d = w_down.shape[1]
    return pl.pallas_call(
        _down_kernel,
        grid=(n_tok // tm, d // tn),
        in_specs=[pl.BlockSpec((tm, f), lambda i, j: (i, 0)),
                  pl.BlockSpec((f, tn), lambda i, j: (0, j)),
                  pl.BlockSpec((tm, tn), lambda i, j: (i, j)),
                  pl.BlockSpec((1, d), lambda i, j: (0, 0))],
        out_specs=pl.BlockSpec((tm, d), lambda i, j: (i, 0)),
        out_shape=jax.ShapeDtypeStruct((n_tok, d), F32),
        scratch_shapes=[pltpu.VMEM((d // tn, tm, tn), F32)],
        compiler_params=_params("parallel", "arbitrary"),
        name="ffn_down",
    )(act, w_down, h, g)


def _rope128(x, cos, sin):
    return x * cos + pltpu.roll(x, 64, 1) * sin


def _rope64(x, cos, sin, lane):
    partner = jnp.where((lane & 63) < 32, pltpu.roll(x, 96, 1), pltpu.roll(x, 32, 1))
    return x * cos + partner * sin


def _attn_front_kernel(x_ref, g_ref, w_ref, c128_ref, s128_ref, c64_ref, s64_ref,
                       q_ref, kf_ref, kb_ref, vf_ref, vb_ref, qi_ref, kif_ref, kib_ref, wi_ref):
    x = x_ref[...]
    y = x * lax.rsqrt(jnp.mean(x * x, axis=-1, keepdims=True) + EPS)
    n = (y * g_ref[...]).astype(BF16)
    p = jnp.dot(n, w_ref[...], preferred_element_type=F32)
    c128, s128 = c128_ref[...], s128_ref[...]
    c64, s64 = c64_ref[...], s64_ref[...]
    tm = x_ref.shape[0]
    lane = lax.broadcasted_iota(I32, (tm, LANES), 1)
    for h in range(N_HEADS):
        q_ref[:, h * 128:(h + 1) * 128] = _rope128(p[:, h * 128:(h + 1) * 128], c128, s128).astype(BF16)
    for h in range(N_KV_HEADS):
        r = _rope128(p[:, 1024 + h * 128:1024 + (h + 1) * 128], c128, s128)
        kf_ref[:, h * 128:(h + 1) * 128] = r
        kb_ref[:, h * 128:(h + 1) * 128] = r.astype(BF16)
    v = p[:, 1280:1536]
    vf_ref[...] = v
    vb_ref[...] = v.astype(BF16)
    for hp in range(IDX_HEADS // 2):
        r = _rope64(p[:, 1536 + hp * 128:1536 + (hp + 1) * 128], c64, s64, lane)
        qi_ref[2 * hp] = r[:, :64].astype(BF16)
        qi_ref[2 * hp + 1] = r[:, 64:].astype(BF16)
    x = p[:, 2560:2688]
    r = _rope64(x, c64, s64, lane)
    kif_ref[...] = r[:, :64]
    kib_ref[...] = r[:, :64].astype(BF16)
    wi_ref[...] = x[:, 64:64 + IDX_HEADS] * (IDX_HEADS ** -0.5 * IDX_DIM ** -0.5)


def attn_front(x, g, w, tabs, n_pos_tiles, tm):
    n_tok, d = x.shape
    tab_spec = pl.BlockSpec((tm, LANES), lambda i: (i % n_pos_tiles, 0))
    row = lambda w: pl.BlockSpec((tm, w), lambda i: (i, 0))
    sds = jax.ShapeDtypeStruct
    return pl.pallas_call(
        _attn_front_kernel,
        grid=(n_tok // tm,),
        in_specs=[pl.BlockSpec((tm, d), lambda i: (i, 0)),
                  pl.BlockSpec((1, d), lambda i: (0, 0)),
                  pl.BlockSpec(w.shape, lambda i: (0, 0))] + [tab_spec] * 4,
        out_specs=[row(1024), row(256), row(256), row(256), row(256),
                   pl.BlockSpec((IDX_HEADS, tm, IDX_DIM), lambda i: (0, i, 0)),
                   row(IDX_DIM), row(IDX_DIM), row(IDX_HEADS)],
        out_shape=[sds((n_tok, 1024), BF16), sds((n_tok, 256), F32), sds((n_tok, 256), BF16),
                   sds((n_tok, 256), F32), sds((n_tok, 256), BF16),
                   sds((IDX_HEADS, n_tok, IDX_DIM), BF16),
                   sds((n_tok, IDX_DIM), F32), sds((n_tok, IDX_DIM), BF16),
                   sds((n_tok, IDX_HEADS), F32)],
        compiler_params=_params("parallel"),
        name="attn_front",
    )(x, g, w, *tabs)


def rope_tables(pos):
    pos = pos.astype(F32)[:, None]

    def tab(half):
        freqs = ROPE_THETA ** (-jnp.arange(half, dtype=F32) / half)
        ang = pos * freqs[None, :]
        c, s = jnp.cos(ang), jnp.sin(ang)
        reps = LANES // (2 * half)
        return jnp.tile(jnp.concatenate([c, c], -1), (1, reps)), jnp.tile(jnp.concatenate([-s, s], -1), (1, reps))

    c128, s128 = tab(HEAD_DIM // 2)
    c64, s64 = tab(IDX_DIM // 2)
    return c128, s128, c64, s64


def _prompt_attn_kernel(q_ref, k_ref, v_ref, qi_ref, ki_ref, wit_ref, o_ref,
                        sc_ref, qs_ref, m_ref, l_ref, acc_ref, *, topk, kc, cpb):
    i = pl.program_id(1)
    tq = q_ref.shape[1]
    nchunk = ((i + 1) * tq + kc - 1) // kc
    gq = GROUP * tq

    def chunk_rows(c):
        return pl.ds(pl.multiple_of(c * kc, kc), kc)

    qi_all = qi_ref[...].reshape(IDX_HEADS * tq, IDX_DIM)
    qpos = i * tq + lax.broadcasted_iota(I32, (kc, tq), 1)

    def score_chunk(c, carry):
        d = lax.dot_general(ki_ref[0, chunk_rows(c), :], qi_all, (((1,), (1,)), ((), ())),
                            preferred_element_type=F32)
        tot = jnp.zeros((kc, tq), F32)
        for h in range(IDX_HEADS):
            tot = tot + wit_ref[0, h:h + 1, :] * jnp.maximum(d[:, h * tq:(h + 1) * tq], 0.0)
        kpos = c * kc + lax.broadcasted_iota(I32, (kc, tq), 0)
        sc_ref[chunk_rows(c), :] = jnp.where(kpos <= qpos, tot, -jnp.inf)
        return carry

    lax.fori_loop(0, nchunk, score_chunk, 0)

    kq = jnp.minimum(topk, i * tq + 1 + lax.broadcasted_iota(I32, (1, tq), 1))
    nacc = 4 * 8

    @pl.when(nchunk % cpb != 0)
    def _():
        sc_ref[chunk_rows(nchunk), :] = jnp.full((kc, tq), -jnp.inf, F32)

    def count_ge(cand):
        def blk(c, acc):
            rows = pl.ds(pl.multiple_of(c * (cpb * kc), cpb * kc), cpb * kc)
            hit = jnp.where(sc_ref[rows, :] >= cand, 1, 0).astype(I32)
            return acc + jnp.sum(hit.reshape(cpb * kc // nacc, nacc, tq), axis=0)
        acc = lax.fori_loop(0, (nchunk + cpb - 1) // cpb, blk, jnp.zeros((nacc, tq), I32))
        return jnp.sum(acc, axis=0, keepdims=True)

    thr = _kth_largest(count_ge, kq)

    for h in range(N_HEADS):
        g, hh = divmod(h, GROUP)
        qs_ref[g, hh * tq:(hh + 1) * tq, :] = q_ref[0, :, h * HEAD_DIM:(h + 1) * HEAD_DIM]
    m_ref[...] = jnp.full(m_ref.shape, NEG_BIG, F32)
    l_ref[...] = jnp.zeros(l_ref.shape, F32)
    acc_ref[...] = jnp.zeros(acc_ref.shape, F32)
    c2 = HEAD_DIM ** -0.5 * LOG2E

    def attn_chunk(c, carry):
        rows = chunk_rows(c)
        bias_t = jnp.where(sc_ref[rows, :] >= thr, 0.0, NEG_BIG).astype(F32)
        bias = jnp.transpose(bias_t)
        for g in range(N_KV_HEADS):
            kg = k_ref[0, rows, g * HEAD_DIM:(g + 1) * HEAD_DIM]
            vg = v_ref[0, rows, g * HEAD_DIM:(g + 1) * HEAD_DIM]
            s = lax.dot_general(qs_ref[g], kg, (((1,), (1,)), ((), ())), preferred_element_type=F32)
            s = (s.reshape(GROUP, tq, kc) + bias[None]).reshape(gq, kc)
            m_old = m_ref[g]
            m_new = jnp.maximum(m_old, jnp.max(s, axis=1, keepdims=True))
            p = jnp.exp2((s - jnp.tile(m_new, (1, kc // LANES))) * c2)
            alpha = jnp.exp2((m_old - m_new) * c2)
            l_ref[g] = alpha * l_ref[g] + jnp.sum(p, axis=1, keepdims=True)
            acc_ref[g] = alpha * acc_ref[g] + jnp.dot(p.astype(BF16), vg, preferred_element_type=F32)
            m_ref[g] = m_new
        return carry

    lax.fori_loop(0, nchunk, attn_chunk, 0)

    for h in range(N_HEADS):
        g, hh = divmod(h, GROUP)
        rs = slice(hh * tq, (hh + 1) * tq)
        o_ref[0, :, h * HEAD_DIM:(h + 1) * HEAD_DIM] = (acc_ref[g, rs, :] / l_ref[g, rs, :]).astype(o_ref.dtype)


def prompt_attention(q, k, v, qi, ki, wit, batch, seq, tq=LANES):
    nt = seq // tq
    kc = min(seq, 512)
    q3 = q.reshape(batch, seq, N_HEADS * HEAD_DIM)
    k3 = k.reshape(batch, seq, N_KV_HEADS * HEAD_DIM)
    v3 = v.reshape(batch, seq, N_KV_HEADS * HEAD_DIM)
    ki3 = ki.reshape(batch, seq, IDX_DIM)
    out = pl.pallas_call(
        functools.partial(_prompt_attn_kernel, topk=min(TOPK_MAX, seq // 4), kc=kc,
                          cpb=2 if seq % (2 * kc) == 0 else 1),
        grid=(batch, nt),
        in_specs=[pl.BlockSpec((1, tq, N_HEADS * HEAD_DIM), lambda b, i: (b, i, 0)),
                  pl.BlockSpec((1, seq, N_KV_HEADS * HEAD_DIM), lambda b, i: (b, 0, 0)),
                  pl.BlockSpec((1, seq, N_KV_HEADS * HEAD_DIM), lambda b, i: (b, 0, 0)),
                  pl.BlockSpec((IDX_HEADS, tq, IDX_DIM), lambda b, i: (0, b * nt + i, 0)),
                  pl.BlockSpec((1, seq, IDX_DIM), lambda b, i: (b, 0, 0)),
                  pl.BlockSpec((1, IDX_HEADS, tq), lambda b, i: (b, 0, i))],
        out_specs=pl.BlockSpec((1, tq, N_HEADS * HEAD_DIM), lambda b, i: (b, i, 0)),
        out_shape=jax.ShapeDtypeStruct((batch, seq, N_HEADS * HEAD_DIM), BF16),
        scratch_shapes=[pltpu.VMEM((seq, tq), F32),
                        pltpu.VMEM((N_KV_HEADS, GROUP * tq, HEAD_DIM), BF16),
                        pltpu.VMEM((N_KV_HEADS, GROUP * tq, LANES), F32),
                        pltpu.VMEM((N_KV_HEADS, GROUP * tq, LANES), F32),
                        pltpu.VMEM((N_KV_HEADS, GROUP * tq, HEAD_DIM), F32)],
        compiler_params=_params("parallel", "arbitrary"),
        name="prompt_attn",
    )(q3, k3, v3, qi, ki3, wit)
    return out.reshape(batch * seq, N_HEADS * HEAD_DIM)


def _conv_kernel(g_ref, gh_ref, prev_ref, dw_ref, b_ref, lg_ref, lb_ref, o_ref, buf_ref, sh_ref, y_ref):
    i = pl.program_id(1)
    tt, ch = g_ref.shape[1], g_ref.shape[2]
    buf_ref[HALO:, :] = g_ref[0]

    @pl.when(i == 0)
    def _():
        buf_ref[:HALO, :] = prev_ref[0]

    @pl.when(i > 0)
    def _():
        buf_ref[:HALO, :] = gh_ref[0]

    off = HALO - (CONV_KERNEL - 1)
    span = sh_ref.shape[1]
    for r in range(1, SUBLANES):
        sh_ref[r - 1] = buf_ref[r:r + span, :]
    rb = min(tt, 64)
    for cb in range(ch // LANES):
        lanes = slice(cb * LANES, (cb + 1) * LANES)
        w = [dw_ref[j:j + 1, lanes] for j in range(CONV_KERNEL)]
        for r0 in range(0, tt, rb):
            acc = jnp.zeros((rb, LANES), F32)
            for j in range(CONV_KERNEL):
                r = (off + j) % SUBLANES
                q = off + j - r + r0
                src = buf_ref if r == 0 else sh_ref.at[r - 1]
                acc = acc + src[q:q + rb, lanes] * w[j]
            y_ref[r0:r0 + rb, lanes] = acc
    y = y_ref[...] + b_ref[...]
    mu = jnp.mean(y, axis=-1, keepdims=True)
    yc = y - mu
    z = yc * lax.rsqrt(jnp.mean(yc * yc, axis=-1, keepdims=True) + EPS)
    z = z * lg_ref[...] + lb_ref[...]
    o_ref[0] = (z * jax.nn.sigmoid(z)).astype(o_ref.dtype)


def conformer_conv(g, prev, dw, bias, ln_g, ln_b, tt):
    batch, seq, ch = g.shape
    per = tt // HALO
    halo_src = g if seq > tt else prev
    return pl.pallas_call(
        _conv_kernel,
        grid=(batch, seq // tt),
        in_specs=[pl.BlockSpec((1, tt, ch), lambda b, i: (b, i, 0)),
                  pl.BlockSpec((1, HALO, ch), lambda b, i: (b, jnp.maximum(i * per - 1, 0), 0)),
                  pl.BlockSpec((1, HALO, ch), lambda b, i: (b, 0, 0)),
                  pl.BlockSpec((HALO, ch), lambda b, i: (0, 0)),
                  pl.BlockSpec((1, ch), lambda b, i: (0, 0)),
                  pl.BlockSpec((1, ch), lambda b, i: (0, 0)),
                  pl.BlockSpec((1, ch), lambda b, i: (0, 0))],
        out_specs=pl.BlockSpec((1, tt, ch), lambda b, i: (b, i, 0)),
        out_shape=jax.ShapeDtypeStruct((batch, seq, ch), BF16),
        scratch_shapes=[pltpu.VMEM((HALO + tt, ch), F32),
                        pltpu.VMEM((SUBLANES - 1, HALO + tt - SUBLANES, ch), F32),
                        pltpu.VMEM((tt, ch), F32)],
        compiler_params=_params("parallel", "arbitrary"),
        name="conformer_conv",
    )(g, halo_src, prev, dw, bias, ln_g, ln_b)


PPS = 16
PPA = 16
QPAD = 8


def _sample_scores_kernel(pt_ref, qi_ref, w_ref, kin_ref, idx_hbm, i_ref, inew_ref, pbuf, sem):
    s = pl.program_id(1)
    nsteps = pl.num_programs(1)
    t = pl.program_id(0) * nsteps + s
    slot = lax.rem(t, 2)
    qi = qi_ref[0]
    w = w_ref[0]

    def page_copies(step, buf_slot):
        return [pltpu.make_async_copy(idx_hbm.at[pt_ref[step * PPS + r]], pbuf.at[buf_slot, r], sem.at[buf_slot])
                for r in range(PPS)]

    @pl.when(t == 0)
    def _():
        for c in page_copies(0, 0):
            c.start()

    @pl.when(t + 1 < pl.num_programs(0) * nsteps)
    def _():
        for c in page_copies(t + 1, 1 - slot):
            c.start()

    for c in page_copies(t, slot):
        c.wait()

    def scores(keys_t):
        d = jnp.dot(qi, keys_t, preferred_element_type=F32)
        e = w * jnp.maximum(d, 0.0)
        return jnp.sum(e.reshape(IDX_HEADS, QPAD, e.shape[1]), axis=0)

    i_ref[0] = scores(jnp.concatenate([pbuf[slot, r].astype(BF16) for r in range(PPS)], axis=1))

    @pl.when(s == 0)
    def _():
        sn = scores(kin_ref[0])
        qrow = lax.broadcasted_iota(I32, sn.shape, 0)
        kcol = lax.broadcasted_iota(I32, sn.shape, 1)
        inew_ref[0] = jnp.where(kcol <= qrow, sn, -jnp.inf)


def sample_scores(page_table, qi, w, ki_new_t, cache_idx_kt):
    bs, n_pages = page_table.shape
    grid_spec = pltpu.PrefetchScalarGridSpec(
        num_scalar_prefetch=1,
        grid=(bs, n_pages // PPS),
        in_specs=[pl.BlockSpec((1, IDX_HEADS * QPAD, IDX_DIM), lambda b, s, pt: (b, 0, 0)),
                  pl.BlockSpec((1, IDX_HEADS * QPAD, 1), lambda b, s, pt: (b, 0, 0)),
                  pl.BlockSpec((1, IDX_DIM, PAGE), lambda b, s, pt: (b, 0, 0)),
                  pl.BlockSpec(memory_space=pl.ANY)],
        out_specs=[pl.BlockSpec((1, QPAD, PPS * PAGE), lambda b, s, pt: (b, 0, s)),
                   pl.BlockSpec((1, QPAD, PAGE), lambda b, s, pt: (b, 0, 0))],
        scratch_shapes=[pltpu.VMEM((2, PPS, IDX_DIM, PAGE), F32),
                        pltpu.SemaphoreType.DMA((2,))],
    )
    return pl.pallas_call(
        _sample_scores_kernel,
        grid_spec=grid_spec,
        out_shape=[jax.ShapeDtypeStruct((bs, QPAD, n_pages * PAGE), F32),
                   jax.ShapeDtypeStruct((bs, QPAD, PAGE), F32)],
        compiler_params=_params("arbitrary", "arbitrary"),
        name="sample_scores",
    )(page_table.reshape(-1), qi, w, ki_new_t, cache_idx_kt)


def _sample_thr_kernel(i_ref, thr_ref, *, k):
    rows, width = i_ref.shape
    nblk = width // LANES

    def count_ge(cand):
        def blk(c, acc):
            s = i_ref[:, pl.ds(pl.multiple_of(c * LANES, LANES), LANES)]
            return acc + jnp.where(s >= cand, 1, 0).astype(I32)
        acc = lax.fori_loop(0, nblk, blk, jnp.zeros((rows, LANES), I32))
        return jnp.sum(acc, axis=1, keepdims=True)

    thr = _kth_largest(count_ge, jnp.full((rows, 1), k, I32))
    thr_ref[...] = jnp.broadcast_to(thr, thr_ref.shape)


def sample_threshold(scores, k):
    rows, width = scores.shape
    return pl.pallas_call(
        functools.partial(_sample_thr_kernel, k=k),
        out_shape=jax.ShapeDtypeStruct((rows, LANES), F32),
        compiler_params=pltpu.CompilerParams(vmem_limit_bytes=VMEM_LIMIT),
        name="sample_threshold",
    )(scores)


def _sample_attn_kernel(pt_ref, q_ref, i_ref, inew_ref, thr_ref, kn_ref, vn_ref, k_hbm, v_hbm,
                        o_ref, m_ref, l_ref, acc_ref, kbuf, vbuf, sem):
    s = pl.program_id(1)
    nsteps = pl.num_programs(1)
    t = pl.program_id(0) * nsteps + s
    slot = lax.rem(t, 2)
    c2 = HEAD_DIM ** -0.5 * LOG2E
    rows_g = GROUP * QPAD
    page_rows = PAGE * N_KV_HEADS
    thr = thr_ref[0]

    def page_copies(step, buf_slot):
        copies = []
        for r in range(PPA):
            page = pt_ref[step * PPA + r]
            src = pl.ds(pl.multiple_of(page * page_rows, page_rows), page_rows)
            dst = pl.ds(r * page_rows, page_rows)
            copies.append(pltpu.make_async_copy(k_hbm.at[src, :], kbuf.at[buf_slot, dst, :], sem.at[0, buf_slot]))
            copies.append(pltpu.make_async_copy(v_hbm.at[src, :], vbuf.at[buf_slot, dst, :], sem.at[1, buf_slot]))
        return copies

    @pl.when(t == 0)
    def _():
        for c in page_copies(0, 0):
            c.start()

    @pl.when(t + 1 < pl.num_programs(0) * nsteps)
    def _():
        for c in page_copies(t + 1, 1 - slot):
            c.start()

    for c in page_copies(t, slot):
        c.wait()

    @pl.when(s == 0)
    def _():
        m_ref[...] = jnp.full(m_ref.shape, NEG_BIG, F32)
        l_ref[...] = jnp.zeros(l_ref.shape, F32)
        acc_ref[...] = jnp.zeros(acc_ref.shape, F32)

    def update(scores_blk, kv_of_group):
        nk = scores_blk.shape[1]
        bias = jnp.where(scores_blk >= jnp.tile(thr, (1, nk // LANES)), 0.0, NEG_BIG).astype(F32)
        for g in range(N_KV_HEADS):
            rs = slice(g * rows_g, (g + 1) * rows_g)
            kg, vg = kv_of_group(g)
            sc = lax.dot_general(q_ref[0, rs, :], kg, (((1,), (1,)), ((), ())), preferred_element_type=F32)
            sc = (sc.reshape(GROUP, QPAD, nk) + bias[None]).reshape(rows_g, nk)
            m_old = m_ref[rs, :]
            m_new = jnp.maximum(m_old, jnp.max(sc, axis=1, keepdims=True))
            p = jnp.exp2((sc - jnp.tile(m_new, (1, nk // LANES))) * c2)
            alpha = jnp.exp2((m_old - m_new) * c2)
            l_ref[rs, :] = alpha * l_ref[rs, :] + jnp.sum(p, axis=1, keepdims=True)
            acc_ref[rs, :] = alpha * acc_ref[rs, :] + jnp.dot(p.astype(BF16), vg, preferred_element_type=F32)
            m_ref[rs, :] = m_new

    def paged_kv(g):
        rows = pl.ds(g, PPA * PAGE, stride=N_KV_HEADS)
        return kbuf[slot, rows, :].astype(BF16), vbuf[slot, rows, :].astype(BF16)

    update(i_ref[0], paged_kv)

    @pl.when(s == pl.num_programs(1) - 1)
    def _():
        update(inew_ref[0], lambda g: (kn_ref[0, :, g * HEAD_DIM:(g + 1) * HEAD_DIM],
                                       vn_ref[0, :, g * HEAD_DIM:(g + 1) * HEAD_DIM]))
        o_ref[0] = (acc_ref[...] / l_ref[...]).astype(o_ref.dtype)


def sample_attention(page_table, q, scores, scores_new, thr, k_new, v_new, cache_k_rows, cache_v_rows):
    bs, n_pages = page_table.shape
    kvw = N_KV_HEADS * HEAD_DIM
    step_rows = PPA * PAGE * N_KV_HEADS
    fixed = lambda shape: pl.BlockSpec(shape, lambda b, s, pt: (b, 0, 0))
    grid_spec = pltpu.PrefetchScalarGridSpec(
        num_scalar_prefetch=1,
        grid=(bs, n_pages // PPA),
        in_specs=[fixed((1, N_HEADS * QPAD, HEAD_DIM)),
                  pl.BlockSpec((1, QPAD, PPA * PAGE), lambda b, s, pt: (b, 0, s)),
                  fixed((1, QPAD, PAGE)), fixed((1, QPAD, LANES)),
                  fixed((1, PAGE, kvw)), fixed((1, PAGE, kvw)),
                  pl.BlockSpec(memory_space=pl.ANY), pl.BlockSpec(memory_space=pl.ANY)],
        out_specs=fixed((1, N_HEADS * QPAD, HEAD_DIM)),
        scratch_shapes=[pltpu.VMEM((N_HEADS * QPAD, LANES), F32),
                        pltpu.VMEM((N_HEADS * QPAD, LANES), F32),
                        pltpu.VMEM((N_HEADS * QPAD, HEAD_DIM), F32),
                        pltpu.VMEM((2, step_rows, HEAD_DIM), F32),
                        pltpu.VMEM((2, step_rows, HEAD_DIM), F32),
                        pltpu.SemaphoreType.DMA((2, 2))],
    )
    return pl.pallas_call(
        _sample_attn_kernel,
        grid_spec=grid_spec,
        out_shape=jax.ShapeDtypeStruct((bs, N_HEADS * QPAD, HEAD_DIM), BF16),
        compiler_params=_params("arbitrary", "arbitrary"),
        name="sample_attn",
    )(page_table.reshape(-1), q, scores, scores_new, thr, k_new, v_new, cache_k_rows, cache_v_rows)


def _glu(a, gate):
    return a * jax.nn.sigmoid(gate)


def _token_tile(n_tok, pref):
    return pref if n_tok % pref == 0 else n_tok


def _layer_front(x2, pos_tabs, n_pos_tiles, wts, tm):
    tf = min(tm, 256)
    prep = attn_front(x2, wts["norm_attn"], wts["w_a"], pos_tabs, n_pos_tiles * (tm // tf), tf)
    g = rms_mm(x2, wts["norm_attn"], [wts["w_u1"], wts["w_u2"]], _glu, F32, tm, wts["w_u1"].shape[1])
    return prep, g


def _layer_back(x2, a, c, wts, tm):
    h, n2 = out_proj(a, c, wts["w_oa"], wts["w_oc"], x2, wts["norm_ffn"], min(tm, 256))
    act = ffn_up(n2, wts["w_gate"], wts["w_up"], tm, 512)
    return ffn_down(act, wts["w_down"], h, wts["norm_final"], tm, 512)


def _prep_weights(l, norm_attn, w_in, conv_dw, conv_dw_bias, conv_ln_g, conv_ln_b, w_out, norm_ffn,
                  w_gate, w_up, w_down, norm_final):
    w = w_in[l]
    aw = N_HEADS * HEAD_DIM
    kvw = N_KV_HEADS * HEAD_DIM
    o_small = aw + 2 * kvw + IDX_HEADS * IDX_DIM
    o_u = o_small + IDX_DIM + IDX_HEADS
    cw = (w.shape[1] - o_u) // 2
    pad = jnp.zeros((w.shape[0], LANES - IDX_DIM - IDX_HEADS), w.dtype)
    w_a = jnp.concatenate([w[:, :o_u], pad], axis=1).astype(BF16)
    dw = jnp.concatenate([conv_dw[l], jnp.zeros((HALO - CONV_KERNEL, cw), F32)], axis=0)
    return {
        "norm_attn": norm_attn[l][None, :], "w_a": w_a,
        "w_u1": w[:, o_u:o_u + cw].astype(BF16), "w_u2": w[:, o_u + cw:].astype(BF16),
        "dw": dw, "dw_bias": conv_dw_bias[l][None, :], "ln_g": conv_ln_g[l][None, :], "ln_b": conv_ln_b[l][None, :],
        "w_oa": w_out[l][:aw].astype(BF16), "w_oc": w_out[l][aw:].astype(BF16),
        "norm_ffn": norm_ffn[l][None, :],
        "w_gate": w_gate[l].astype(BF16), "w_up": w_up[l].astype(BF16), "w_down": w_down[l].astype(BF16),
        "norm_final": norm_final[None, :],
    }


def kernel(x_prompt, x_sample, cache_k, cache_v, cache_idx_k, state_conv, page_table, norm_attn, w_in, conv_dw, conv_dw_bias, conv_ln_g, conv_ln_b, w_out, norm_ffn, w_gate, w_up, w_down, norm_final):
    batch, seq, d_model = x_prompt.shape
    bs, ts = x_sample.shape[:2]
    past = page_table.shape[1] * PAGE
    cw = conv_dw.shape[2]
    assert norm_attn.shape[0] == 1, "single-layer step"
    wts = _prep_weights(0, norm_attn, w_in, conv_dw, conv_dw_bias, conv_ln_g, conv_ln_b, w_out, norm_ffn,
                        w_gate, w_up, w_down, norm_final)

    tm = _token_tile(batch * seq, 512)
    xp = x_prompt.reshape(batch * seq, d_model)
    tabs_p = rope_tables(jnp.arange(seq, dtype=jnp.int32))
    (q, kf, kb, vf, vb, qi, kif, kib, wi), g = _layer_front(xp, tabs_p, seq // tm, wts, tm)
    wit = wi.reshape(batch, seq, IDX_HEADS).transpose(0, 2, 1)
    a = prompt_attention(q, kb, vb, qi, kib, wit, batch, seq)
    g3 = g.reshape(batch, seq, cw)
    c = conformer_conv(g3, jnp.zeros((batch, HALO, cw), F32), wts["dw"], wts["dw_bias"], wts["ln_g"], wts["ln_b"],
                       min(seq, 256))
    y_prompt = _layer_back(xp, a, c.reshape(batch * seq, cw), wts, tm).reshape(batch, seq, d_model)
    k_prompt = kf.reshape(1, batch, seq, N_KV_HEADS, HEAD_DIM)
    v_prompt = vf.reshape(1, batch, seq, N_KV_HEADS, HEAD_DIM)
    idx_k_prompt = kif.reshape(1, batch, seq, IDX_DIM)
    conv_prompt = g3[:, seq - (CONV_KERNEL - 1):][None]

    ns = bs * ts
    xs = x_sample.reshape(ns, d_model)
    tabs_s = tuple(jnp.tile(t, (bs, 1)) for t in rope_tables(past + jnp.arange(ts, dtype=jnp.int32)))
    (q, kf, kb, vf, vb, qi, kif, kib, wi), g = _layer_front(xs, tabs_s, 1, wts, ns)

    def pad_q(t):
        return jnp.pad(t, ((0, 0), (0, QPAD - ts)) + ((0, 0),) * (t.ndim - 2))

    def pad_keys(t):
        return jnp.pad(t, ((0, 0), (0, PAGE - ts), (0, 0)))

    qi_s = pad_q(qi.reshape(IDX_HEADS, bs, ts, IDX_DIM).transpose(1, 2, 0, 3))
    qi_s = qi_s.transpose(0, 2, 1, 3).reshape(bs, IDX_HEADS * QPAD, IDX_DIM)
    w_s = pad_q(wi.reshape(bs, ts, IDX_HEADS)).transpose(0, 2, 1).reshape(bs, IDX_HEADS * QPAD, 1)
    ki_new_t = jnp.swapaxes(pad_keys(kib.reshape(bs, ts, IDX_DIM)), 1, 2)
    sc_past, sc_new = sample_scores(page_table, qi_s, w_s, ki_new_t, jnp.swapaxes(cache_idx_k[0], 1, 2))
    sc_all = jnp.concatenate([sc_past[:, :ts], sc_new[:, :ts]], axis=-1).reshape(ns, past + PAGE)
    thr = sample_threshold(sc_all, min(TOPK_MAX, (past + ts) // 4))
    thr = pad_q(thr.reshape(bs, ts, LANES))
    q_s = pad_q(q.reshape(bs, ts, N_HEADS, HEAD_DIM)).transpose(0, 2, 1, 3).reshape(bs, N_HEADS * QPAD, HEAD_DIM)
    kvw = N_KV_HEADS * HEAD_DIM
    o = sample_attention(page_table, q_s, sc_past, sc_new, thr,
                         pad_keys(kb.reshape(bs, ts, kvw)), pad_keys(vb.reshape(bs, ts, kvw)),
                         cache_k[0].reshape(-1, HEAD_DIM), cache_v[0].reshape(-1, HEAD_DIM))
    a = o.reshape(bs, N_HEADS, QPAD, HEAD_DIM)[:, :, :ts].transpose(0, 2, 1, 3).reshape(ns, N_HEADS * HEAD_DIM)

    g3 = g.reshape(bs, ts, cw)
    prev = state_conv[0]
    prev_pad = jnp.pad(prev, ((0, 0), (HALO - (CONV_KERNEL - 1), 0), (0, 0)))
    c = conformer_conv(pad_q(g3), prev_pad, wts["dw"], wts["dw_bias"], wts["ln_g"], wts["ln_b"], QPAD)
    y_sample = _layer_back(xs, a, c[:, :ts].reshape(ns, cw), wts, ns).reshape(bs, ts, d_model)
    k_sample = kf.reshape(1, bs, ts, N_KV_HEADS, HEAD_DIM)
    v_sample = vf.reshape(1, bs, ts, N_KV_HEADS, HEAD_DIM)
    idx_k_sample = kif.reshape(1, bs, ts, IDX_DIM)
    conv_sample = jnp.concatenate([prev, g3], axis=1)[:, ts:][None]

    return (y_prompt, y_sample, k_prompt, v_prompt, idx_k_prompt, conv_prompt,
            k_sample, v_sample, idx_k_sample, conv_sample)
```

```python
import functools

import numpy as np
import jax
import jax.numpy as jnp
from jax import lax
from jax.experimental import pallas as pl
from jax.experimental.pallas import tpu as pltpu

N_HEADS = 8
HEAD_DIM = 128
N_KV_HEADS = 2
GROUP = N_HEADS // N_KV_HEADS
IDX_HEADS = 16
IDX_DIM = 64
TOPK_MAX = 256
ROPE_THETA = 10000.0
CONV_KERNEL = 31
EPS = 1e-6
PAGE = 128

LANES = 128
SUBLANES = 8
HALO = 32
VMEM_LIMIT = 56 * 1024 * 1024
INT_MIN = np.int32(-2 ** 31)
NEG_BIG = -1e30
LOG2E = 1.4426950408889634

BF16 = jnp.bfloat16
F32 = jnp.float32
I32 = jnp.int32


def _params(*sem):
    return pltpu.CompilerParams(dimension_semantics=sem, vmem_limit_bytes=VMEM_LIMIT)


def _rank_to_float(rank):
    key = rank ^ INT_MIN
    return pltpu.bitcast(key ^ ((key >> 31) & np.int32(0x7FFFFFFF)), F32)


def _kth_largest(count_ge, k):
    def body(it, acc):
        bit = lax.shift_left(np.int32(1), np.int32(31) - it)
        cand = acc | bit
        return jnp.where(count_ge(_rank_to_float(cand)) >= k, cand, acc)
    acc = lax.fori_loop(0, 32, body, jnp.zeros_like(k))
    return _rank_to_float(acc)


def _rms_mm_kernel(x_ref, g_ref, *refs, n_w, combine):
    w_refs, o_ref, n_ref = refs[:n_w], refs[n_w], refs[n_w + 1]

    @pl.when(pl.program_id(1) == 0)
    def _():
        x = x_ref[...]
        y = x * lax.rsqrt(jnp.mean(x * x, axis=-1, keepdims=True) + EPS)
        n_ref[...] = (y * g_ref[...]).astype(BF16)

    n = n_ref[...]
    outs = [jnp.dot(n, w[...], preferred_element_type=F32) for w in w_refs]
    o_ref[...] = combine(*outs).astype(o_ref.dtype)


def rms_mm(x, g, ws, combine, out_dtype, tm, tn):
    n_tok, d = x.shape
    f = ws[0].shape[1]
    kern = functools.partial(_rms_mm_kernel, n_w=len(ws), combine=combine)
    return pl.pallas_call(
        kern,
        grid=(n_tok // tm, f // tn),
        in_specs=[pl.BlockSpec((tm, d), lambda i, j: (i, 0)),
                  pl.BlockSpec((1, d), lambda i, j: (0, 0))]
                 + [pl.BlockSpec((d, tn), lambda i, j: (0, j)) for _ in ws],
        out_specs=pl.BlockSpec((tm, tn), lambda i, j: (i, j)),
        out_shape=jax.ShapeDtypeStruct((n_tok, f), out_dtype),
        scratch_shapes=[pltpu.VMEM((tm, d), BF16)],
        compiler_params=_params("parallel", "arbitrary"),
        name="rms_mm",
    )(x, g, *ws)


def _dual_mm_kernel(n_ref, w1_ref, w2_ref, o_ref):
    n = n_ref[...]
    a = jnp.dot(n, w1_ref[...], preferred_element_type=F32)
    b = jnp.dot(n, w2_ref[...], preferred_element_type=F32)
    o_ref[...] = (a * jax.nn.sigmoid(a) * b).astype(o_ref.dtype)


def ffn_up(n, w_gate, w_up, tm, tn):
    n_tok, d = n.shape
    f = w_gate.shape[1]
    return pl.pallas_call(
        _dual_mm_kernel,
        grid=(n_tok // tm, f // tn),
        in_specs=[pl.BlockSpec((tm, d), lambda i, j: (i, 0)),
                  pl.BlockSpec((d, tn), lambda i, j: (0, j)),
                  pl.BlockSpec((d, tn), lambda i, j: (0, j))],
        out_specs=pl.BlockSpec((tm, tn), lambda i, j: (i, j)),
        out_shape=jax.ShapeDtypeStruct((n_tok, f), BF16),
        compiler_params=_params("parallel", "arbitrary"),
        name="ffn_up",
    )(n, w_gate, w_up)


def _out_proj_kernel(a_ref, c_ref, wa_ref, wc_ref, x_ref, g_ref, h_ref, n2_ref):
    h = x_ref[...]
    h = h + jnp.dot(a_ref[...], wa_ref[...], preferred_element_type=F32)
    h = h + jnp.dot(c_ref[...], wc_ref[...], preferred_element_type=F32)
    h_ref[...] = h
    y = h * lax.rsqrt(jnp.mean(h * h, axis=-1, keepdims=True) + EPS)
    n2_ref[...] = (y * g_ref[...]).astype(BF16)


def out_proj(a, c, wa, wc, x, g, tm):
    n_tok, d = x.shape
    ka, kc = a.shape[1], c.shape[1]
    return pl.pallas_call(
        _out_proj_kernel,
        grid=(n_tok // tm,),
        in_specs=[pl.BlockSpec((tm, ka), lambda i: (i, 0)),
                  pl.BlockSpec((tm, kc), lambda i: (i, 0)),
                  pl.BlockSpec((ka, d), lambda i: (0, 0)),
                  pl.BlockSpec((kc, d), lambda i: (0, 0)),
                  pl.BlockSpec((tm, d), lambda i: (i, 0)),
                  pl.BlockSpec((1, d), lambda i: (0, 0))],
        out_specs=[pl.BlockSpec((tm, d), lambda i: (i, 0)),
                   pl.BlockSpec((tm, d), lambda i: (i, 0))],
        out_shape=[jax.ShapeDtypeStruct((n_tok, d), F32),
                   jax.ShapeDtypeStruct((n_tok, d), BF16)],
        compiler_params=_params("parallel"),
        name="out_proj",
    )(a, c, wa, wc, x, g)


def _down_kernel(a_ref, w_ref, h_ref, g_ref, y_ref, cols_ref):
    j = pl.program_id(1)
    cols_ref[j] = h_ref[...] + jnp.dot(a_ref[...], w_ref[...], preferred_element_type=F32)

    @pl.when(j == pl.num_programs(1) - 1)
    def _():
        h = jnp.concatenate([cols_ref[c] for c in range(cols_ref.shape[0])], axis=1)
        y = h * lax.rsqrt(jnp.mean(h * h, axis=-1, keepdims=True) + EPS)
        y_ref[...] = y * g_ref[...]


def ffn_down(act, w_down, h, g, tm, tn):
    n_tok, f = act.shape
    d = w_down.shape[1]
    return pl.pallas_call(
        _down_kernel,
        grid=(n_tok // tm, d // tn),
        in_specs=[pl.BlockSpec((tm, f), lambda i, j: (i, 0)),
                  pl.BlockSpec((f, tn), lambda i, j: (0, j)),
                  pl.BlockSpec((tm, tn), lambda i, j: (i, j)),
                  pl.BlockSpec((1, d), lambda i, j: (0, 0))],
        out_specs=pl.BlockSpec((tm, d), lambda i, j: (i, 0)),
        out_shape=jax.ShapeDtypeStruct((n_tok, d), F32),
        scratch_shapes=[pltpu.VMEM((d // tn, tm, tn), F32)],
        compiler_params=_params("parallel", "arbitrary"),
        name="ffn_down",
    )(act, w_down, h, g)


def _rope128(x, cos, sin):
    return x * cos + pltpu.roll(x, 64, 1) * sin


def _rope64(x, cos, sin, lane):
    partner = jnp.where((lane & 63) < 32, pltpu.roll(x, 96, 1), pltpu.roll(x, 32, 1))
    return x * cos + partner * sin


def _attn_front_kernel(x_ref, g_ref, w_ref, c128_ref, s128_ref, c64_ref, s64_ref,
                       q_ref, kf_ref, kb_ref, vf_ref, vb_ref, qi_ref, kif_ref, kib_ref, wi_ref):
    x = x_ref[...]
    y = x * lax.rsqrt(jnp.mean(x * x, axis=-1, keepdims=True) + EPS)
    n = (y * g_ref[...]).astype(BF16)
    p = jnp.dot(n, w_ref[...], preferred_element_type=F32)
    c128, s128 = c128_ref[...], s128_ref[...]
    c64, s64 = c64_ref[...], s64_ref[...]
    tm = x_ref.shape[0]
    lane = lax.broadcasted_iota(I32, (tm, LANES), 1)
    for h in range(N_HEADS):
        q_ref[:, h * 128:(h + 1) * 128] = _rope128(p[:, h * 128:(h + 1) * 128], c128, s128).astype(BF16)
    for h in range(N_KV_HEADS):
        r = _rope128(p[:, 1024 + h * 128:1024 + (h + 1) * 128], c128, s128)
        kf_ref[:, h * 128:(h + 1) * 128] = r
        kb_ref[:, h * 128:(h + 1) * 128] = r.astype(BF16)
    v = p[:, 1280:1536]
    vf_ref[...] = v
    vb_ref[...] = v.astype(BF16)
    for hp in range(IDX_HEADS // 2):
        r = _rope64(p[:, 1536 + hp * 128:1536 + (hp + 1) * 128], c64, s64, lane)
        qi_ref[2 * hp] = r[:, :64].astype(BF16)
        qi_ref[2 * hp + 1] = r[:, 64:].astype(BF16)
    x = p[:, 2560:2688]
    r = _rope64(x, c64, s64, lane)
    kif_ref[...] = r[:, :64]
    kib_ref[...] = r[:, :64].astype(BF16)
    wi_ref[...] = x[:, 64:64 + IDX_HEADS] * (IDX_HEADS ** -0.5 * IDX_DIM ** -0.5)


def attn_front(x, g, w, tabs, n_pos_tiles, tm):
    n_tok, d = x.shape
    tab_spec = pl.BlockSpec((tm, LANES), lambda i: (i % n_pos_tiles, 0))
    row = lambda w: pl.BlockSpec((tm, w), lambda i: (i, 0))
    sds = jax.ShapeDtypeStruct
    return pl.pallas_call(
        _attn_front_kernel,
        grid=(n_tok // tm,),
        in_specs=[pl.BlockSpec((tm, d), lambda i: (i, 0)),
                  pl.BlockSpec((1, d), lambda i: (0, 0)),
                  pl.BlockSpec(w.shape, lambda i: (0, 0))] + [tab_spec] * 4,
        out_specs=[row(1024), row(256), row(256), row(256), row(256),
                   pl.BlockSpec((IDX_HEADS, tm, IDX_DIM), lambda i: (0, i, 0)),
                   row(IDX_DIM), row(IDX_DIM), row(IDX_HEADS)],
        out_shape=[sds((n_tok, 1024), BF16), sds((n_tok, 256), F32), sds((n_tok, 256), BF16),
                   sds((n_tok, 256), F32), sds((n_tok, 256), BF16),
                   sds((IDX_HEADS, n_tok, IDX_DIM), BF16),
                   sds((n_tok, IDX_DIM), F32), sds((n_tok, IDX_DIM), BF16),
                   sds((n_tok, IDX_HEADS), F32)],
        compiler_params=_params("parallel"),
        name="attn_front",
    )(x, g, w, *tabs)


def rope_tables(pos):
    pos = pos.astype(F32)[:, None]

    def tab(half):
        freqs = ROPE_THETA ** (-jnp.arange(half, dtype=F32) / half)
        ang = pos * freqs[None, :]
        c, s = jnp.cos(ang), jnp.sin(ang)
        reps = LANES // (2 * half)
        return jnp.tile(jnp.concatenate([c, c], -1), (1, reps)), jnp.tile(jnp.concatenate([-s, s], -1), (1, reps))

    c128, s128 = tab(HEAD_DIM // 2)
    c64, s64 = tab(IDX_DIM // 2)
    return c128, s128, c64, s64


def _prompt_attn_kernel(q_ref, k_ref, v_ref, qi_ref, ki_ref, wit_ref, o_ref,
                        sc_ref, qs_ref, m_ref, l_ref, acc_ref, *, topk, kc, cpb):
    i = pl.program_id(1)
    tq = q_ref.shape[1]
    nchunk = ((i + 1) * tq + kc - 1) // kc
    gq = GROUP * tq

    def chunk_rows(c):
        return pl.ds(pl.multiple_of(c * kc, kc), kc)

    qi_all = qi_ref[...].reshape(IDX_HEADS * tq, IDX_DIM)
    qpos = i * tq + lax.broadcasted_iota(I32, (kc, tq), 1)

    def score_chunk(c, carry):
        d = lax.dot_general(ki_ref[0, chunk_rows(c), :], qi_all, (((1,), (1,)), ((), ())),
                            preferred_element_type=F32)
        tot = jnp.zeros((kc, tq), F32)
        for h in range(IDX_HEADS):
            tot = tot + wit_ref[0, h:h + 1, :] * jnp.maximum(d[:, h * tq:(h + 1) * tq], 0.0)
        kpos = c * kc + lax.broadcasted_iota(I32, (kc, tq), 0)
        sc_ref[chunk_rows(c), :] = jnp.where(kpos <= qpos, tot, -jnp.inf)
        return carry

    lax.fori_loop(0, nchunk, score_chunk, 0)

    kq = jnp.minimum(topk, i * tq + 1 + lax.broadcasted_iota(I32, (1, tq), 1))
    nacc = 4 * 8

    @pl.when(nchunk % cpb != 0)
    def _():
        sc_ref[chunk_rows(nchunk), :] = jnp.full((kc, tq), -jnp.inf, F32)

    def count_ge(cand):
        def blk(c, acc):
            rows = pl.ds(pl.multiple_of(c * (cpb * kc), cpb * kc), cpb * kc)
            hit = jnp.where(sc_ref[rows, :] >= cand, 1, 0).astype(I32)
            return acc + jnp.sum(hit.reshape(cpb * kc // nacc, nacc, tq), axis=0)
        acc = lax.fori_loop(0, (nchunk + cpb - 1) // cpb, blk, jnp.zeros((nacc, tq), I32))
        return jnp.sum(acc, axis=0, keepdims=True)

    thr = _kth_largest(count_ge, kq)

    for h in range(N_HEADS):
        g, hh = divmod(h, GROUP)
        qs_ref[g, hh * tq:(hh + 1) * tq, :] = q_ref[0, :, h * HEAD_DIM:(h + 1) * HEAD_DIM]
    m_ref[...] = jnp.full(m_ref.shape, NEG_BIG, F32)
    l_ref[...] = jnp.zeros(l_ref.shape, F32)
    acc_ref[...] = jnp.zeros(acc_ref.shape, F32)
    c2 = HEAD_DIM ** -0.5 * LOG2E

    def attn_chunk(c, carry):
        rows = chunk_rows(c)
        bias_t = jnp.where(sc_ref[rows, :] >= thr, 0.0, NEG_BIG).astype(F32)
        bias = jnp.transpose(bias_t)
        for g in range(N_KV_HEADS):
            kg = k_ref[0, rows, g * HEAD_DIM:(g + 1) * HEAD_DIM]
            vg = v_ref[0, rows, g * HEAD_DIM:(g + 1) * HEAD_DIM]
            s = lax.dot_general(qs_ref[g], kg, (((1,), (1,)), ((), ())), preferred_element_type=F32)
            s = (s.reshape(GROUP, tq, kc) + bias[None]).reshape(gq, kc)
            m_old = m_ref[g]
            m_new = jnp.maximum(m_old, jnp.max(s, axis=1, keepdims=True))
            p = jnp.exp2((s - jnp.tile(m_new, (1, kc // LANES))) * c2)
            alpha = jnp.exp2((m_old - m_new) * c2)
            l_ref[g] = alpha * l_ref[g] + jnp.sum(p, axis=1, keepdims=True)
            acc_ref[g] = alpha * acc_ref[g] + jnp.dot(p.astype(BF16), vg, preferred_element_type=F32)
            m_ref[g] = m_new
        return carry

    lax.fori_loop(0, nchunk, attn_chunk, 0)

    for h in range(N_HEADS):
        g, hh = divmod(h, GROUP)
        rs = slice(hh * tq, (hh + 1) * tq)
        o_ref[0, :, h * HEAD_DIM:(h + 1) * HEAD_DIM] = (acc_ref[g, rs, :] / l_ref[g, rs, :]).astype(o_ref.dtype)


def prompt_attention(q, k, v, qi, ki, wit, batch, seq, tq=LANES):
    nt = seq // tq
    kc = min(seq, 512)
    q3 = q.reshape(batch, seq, N_HEADS * HEAD_DIM)
    k3 = k.reshape(batch, seq, N_KV_HEADS * HEAD_DIM)
    v3 = v.reshape(batch, seq, N_KV_HEADS * HEAD_DIM)
    ki3 = ki.reshape(batch, seq, IDX_DIM)
    out = pl.pallas_call(
        functools.partial(_prompt_attn_kernel, topk=min(TOPK_MAX, seq // 4), kc=kc,
                          cpb=2 if seq % (2 * kc) == 0 else 1),
        grid=(batch, nt),
        in_specs=[pl.BlockSpec((1, tq, N_HEADS * HEAD_DIM), lambda b, i: (b, i, 0)),
                  pl.BlockSpec((1, seq, N_KV_HEADS * HEAD_DIM), lambda b, i: (b, 0, 0)),
                  pl.BlockSpec((1, seq, N_KV_HEADS * HEAD_DIM), lambda b, i: (b, 0, 0)),
                  pl.BlockSpec((IDX_HEADS, tq, IDX_DIM), lambda b, i: (0, b * nt + i, 0)),
                  pl.BlockSpec((1, seq, IDX_DIM), lambda b, i: (b, 0, 0)),
                  pl.BlockSpec((1, IDX_HEADS, tq), lambda b, i: (b, 0, i))],
        out_specs=pl.BlockSpec((1, tq, N_HEADS * HEAD_DIM), lambda b, i: (b, i, 0)),
        out_shape=jax.ShapeDtypeStruct((batch, seq, N_HEADS * HEAD_DIM), BF16),
        scratch_shapes=[pltpu.VMEM((seq, tq), F32),
                        pltpu.VMEM((N_KV_HEADS, GROUP * tq, HEAD_DIM), BF16),
                        pltpu.VMEM((N_KV_HEADS, GROUP * tq, LANES), F32),
                        pltpu.VMEM((N_KV_HEADS, GROUP * tq, LANES), F32),
                        pltpu.VMEM((N_KV_HEADS, GROUP * tq, HEAD_DIM), F32)],
        compiler_params=_params("parallel", "arbitrary"),
        name="prompt_attn",
    )(q3, k3, v3, qi, ki3, wit)
    return out.reshape(batch * seq, N_HEADS * HEAD_DIM)


def _conv_kernel(g_ref, gh_ref, prev_ref, dw_ref, b_ref, lg_ref, lb_ref, o_ref, buf_ref, sh_ref, y_ref):
    i = pl.program_id(1)
    tt, ch = g_ref.shape[1], g_ref.shape[2]
    buf_ref[HALO:, :] = g_ref[0]

    @pl.when(i == 0)
    def _():
        buf_ref[:HALO, :] = prev_ref[0]

    @pl.when(i > 0)
    def _():
        buf_ref[:HALO, :] = gh_ref[0]

    off = HALO - (CONV_KERNEL - 1)
    span = sh_ref.shape[1]
    for r in range(1, SUBLANES):
        sh_ref[r - 1] = buf_ref[r:r + span, :]
    rb = min(tt, 64)
    for cb in range(ch // LANES):
        lanes = slice(cb * LANES, (cb + 1) * LANES)
        w = [dw_ref[j:j + 1, lanes] for j in range(CONV_KERNEL)]
        for r0 in range(0, tt, rb):
            acc = jnp.zeros((rb, LANES), F32)
            for j in range(CONV_KERNEL):
                r = (off + j) % SUBLANES
                q = off + j - r + r0
                src = buf_ref if r == 0 else sh_ref.at[r - 1]
                acc = acc + src[q:q + rb, lanes] * w[j]
            y_ref[r0:r0 + rb, lanes] = acc
    y = y_ref[...] + b_ref[...]
    mu = jnp.mean(y, axis=-1, keepdims=True)
    yc = y - mu
    z = yc * lax.rsqrt(jnp.mean(yc * yc, axis=-1, keepdims=True) + EPS)
    z = z * lg_ref[...] + lb_ref[...]
    o_ref[0] = (z * jax.nn.sigmoid(z)).astype(o_ref.dtype)


def conformer_conv(g, prev, dw, bias, ln_g, ln_b, tt):
    batch, seq, ch = g.shape
    per = tt // HALO
    halo_src = g if seq > tt else prev
    return pl.pallas_call(
        _conv_kernel,
        grid=(batch, seq // tt),
        in_specs=[pl.BlockSpec((1, tt, ch), lambda b, i: (b, i, 0)),
                  pl.BlockSpec((1, HALO, ch), lambda b, i: (b, jnp.maximum(i * per - 1, 0), 0)),
                  pl.BlockSpec((1, HALO, ch), lambda b, i: (b, 0, 0)),
                  pl.BlockSpec((HALO, ch), lambda b, i: (0, 0)),
                  pl.BlockSpec((1, ch), lambda b, i: (0, 0)),
                  pl.BlockSpec((1, ch), lambda b, i: (0, 0)),
                  pl.BlockSpec((1, ch), lambda b, i: (0, 0))],
        out_specs=pl.BlockSpec((1, tt, ch), lambda b, i: (b, i, 0)),
        out_shape=jax.ShapeDtypeStruct((batch, seq, ch), BF16),
        scratch_shapes=[pltpu.VMEM((HALO + tt, ch), F32),
                        pltpu.VMEM((SUBLANES - 1, HALO + tt - SUBLANES, ch), F32),
                        pltpu.VMEM((tt, ch), F32)],
        compiler_params=_params("parallel", "arbitrary"),
        name="conformer_conv",
    )(g, halo_src, prev, dw, bias, ln_g, ln_b)


PPS = 16
PPA = 16
QPAD = 8


def _sample_scores_kernel(pt_ref, qi_ref, w_ref, kin_ref, idx_hbm, i_ref, inew_ref, pbuf, sem):
    s = pl.program_id(1)
    nsteps = pl.num_programs(1)
    t = pl.program_id(0) * nsteps + s
    slot = lax.rem(t, 2)
    qi = qi_ref[0]
    w = w_ref[0]

    def page_copies(step, buf_slot):
        return [pltpu.make_async_copy(idx_hbm.at[pt_ref[step * PPS + r]], pbuf.at[buf_slot, r], sem.at[buf_slot])
                for r in range(PPS)]

    @pl.when(t == 0)
    def _():
        for r, c in enumerate(page_copies(0, 0)):
            c.start(priority=r % 2)

    @pl.when(t + 1 < pl.num_programs(0) * nsteps)
    def _():
        for r, c in enumerate(page_copies(t + 1, 1 - slot)):
            c.start(priority=r % 2)

    for c in page_copies(t, slot):
        c.wait()

    def scores(keys_t):
        d = jnp.dot(qi, keys_t, preferred_element_type=F32)
        e = w * jnp.maximum(d, 0.0)
        return jnp.sum(e.reshape(IDX_HEADS, QPAD, e.shape[1]), axis=0)

    i_ref[0] = scores(jnp.concatenate([pbuf[slot, r].astype(BF16) for r in range(PPS)], axis=1))

    @pl.when(s == 0)
    def _():
        sn = scores(kin_ref[0])
        qrow = lax.broadcasted_iota(I32, sn.shape, 0)
        kcol = lax.broadcasted_iota(I32, sn.shape, 1)
        inew_ref[0] = jnp.where(kcol <= qrow, sn, -jnp.inf)


def sample_scores(page_table, qi, w, ki_new_t, cache_idx_kt):
    bs, n_pages = page_table.shape
    grid_spec = pltpu.PrefetchScalarGridSpec(
        num_scalar_prefetch=1,
        grid=(bs, n_pages // PPS),
        in_specs=[pl.BlockSpec((1, IDX_HEADS * QPAD, IDX_DIM), lambda b, s, pt: (b, 0, 0)),
                  pl.BlockSpec((1, IDX_HEADS * QPAD, 1), lambda b, s, pt: (b, 0, 0)),
                  pl.BlockSpec((1, IDX_DIM, PAGE), lambda b, s, pt: (b, 0, 0)),
                  pl.BlockSpec(memory_space=pl.ANY)],
        out_specs=[pl.BlockSpec((1, QPAD, PPS * PAGE), lambda b, s, pt: (b, 0, s)),
                   pl.BlockSpec((1, QPAD, PAGE), lambda b, s, pt: (b, 0, 0))],
        scratch_shapes=[pltpu.VMEM((2, PPS, IDX_DIM, PAGE), F32),
                        pltpu.SemaphoreType.DMA((2,))],
    )
    return pl.pallas_call(
        _sample_scores_kernel,
        grid_spec=grid_spec,
        out_shape=[jax.ShapeDtypeStruct((bs, QPAD, n_pages * PAGE), F32),
                   jax.ShapeDtypeStruct((bs, QPAD, PAGE), F32)],
        compiler_params=_params("arbitrary", "arbitrary"),
        name="sample_scores",
    )(page_table.reshape(-1), qi, w, ki_new_t, cache_idx_kt)


def _sample_thr_kernel(i_ref, thr_ref, *, k):
    rows, width = i_ref.shape
    nblk = width // LANES

    def count_ge(cand):
        def blk(c, acc):
            s = i_ref[:, pl.ds(pl.multiple_of(c * LANES, LANES), LANES)]
            return acc + jnp.where(s >= cand, 1, 0).astype(I32)
        acc = lax.fori_loop(0, nblk, blk, jnp.zeros((rows, LANES), I32))
        return jnp.sum(acc, axis=1, keepdims=True)

    thr = _kth_largest(count_ge, jnp.full((rows, 1), k, I32))
    thr_ref[...] = jnp.broadcast_to(thr, thr_ref.shape)


def sample_threshold(scores, k):
    rows, width = scores.shape
    return pl.pallas_call(
        functools.partial(_sample_thr_kernel, k=k),
        out_shape=jax.ShapeDtypeStruct((rows, LANES), F32),
        compiler_params=pltpu.CompilerParams(vmem_limit_bytes=VMEM_LIMIT),
        name="sample_threshold",
    )(scores)


def _sample_attn_kernel(pt_ref, q_ref, i_ref, inew_ref, thr_ref, kn_ref, vn_ref, k_hbm, v_hbm,
                        o_ref, m_ref, l_ref, acc_ref, kbuf, vbuf, sem):
    s = pl.program_id(1)
    nsteps = pl.num_programs(1)
    t = pl.program_id(0) * nsteps + s
    slot = lax.rem(t, 2)
    c2 = HEAD_DIM ** -0.5 * LOG2E
    rows_g = GROUP * QPAD
    page_rows = PAGE * N_KV_HEADS
    thr = thr_ref[0]

    def page_copies(step, buf_slot):
        copies = []
        for r in range(PPA):
            page = pt_ref[step * PPA + r]
            src = pl.ds(pl.multiple_of(page * page_rows, page_rows), page_rows)
            dst = pl.ds(r * page_rows, page_rows)
            copies.append(pltpu.make_async_copy(k_hbm.at[src, :], kbuf.at[buf_slot, dst, :], sem.at[0, buf_slot]))
            copies.append(pltpu.make_async_copy(v_hbm.at[src, :], vbuf.at[buf_slot, dst, :], sem.at[1, buf_slot]))
        return copies

    @pl.when(t == 0)
    def _():
        for c in page_copies(0, 0):
            c.start()

    @pl.when(t + 1 < pl.num_programs(0) * nsteps)
    def _():
        for c in page_copies(t + 1, 1 - slot):
            c.start()

    for c in page_copies(t, slot):
        c.wait()

    @pl.when(s == 0)
    def _():
        m_ref[...] = jnp.full(m_ref.shape, NEG_BIG, F32)
        l_ref[...] = jnp.zeros(l_ref.shape, F32)
        acc_ref[...] = jnp.zeros(acc_ref.shape, F32)

    def update(scores_blk, kv_of_group):
        nk = scores_blk.shape[1]
        bias = jnp.where(scores_blk >= jnp.tile(thr, (1, nk // LANES)), 0.0, NEG_BIG).astype(F32)
        for g in range(N_KV_HEADS):
            rs = slice(g * rows_g, (g + 1) * rows_g)
            kg, vg = kv_of_group(g)
            sc = lax.dot_general(q_ref[0, rs, :], kg, (((1,), (1,)), ((), ())), preferred_element_type=F32)
            sc = (sc.reshape(GROUP, QPAD, nk) + bias[None]).reshape(rows_g, nk)
            m_old = m_ref[rs, :]
            m_new = jnp.maximum(m_old, jnp.max(sc, axis=1, keepdims=True))
            p = jnp.exp2((sc - jnp.tile(m_new, (1, nk // LANES))) * c2)
            alpha = jnp.exp2((m_old - m_new) * c2)
            l_ref[rs, :] = alpha * l_ref[rs, :] + jnp.sum(p, axis=1, keepdims=True)
            acc_ref[rs, :] = alpha * acc_ref[rs, :] + jnp.dot(p.astype(BF16), vg, preferred_element_type=F32)
            m_ref[rs, :] = m_new

    def paged_kv(g):
        rows = pl.ds(g, PPA * PAGE, stride=N_KV_HEADS)
        return kbuf[slot, rows, :].astype(BF16), vbuf[slot, rows, :].astype(BF16)

    update(i_ref[0], paged_kv)

    @pl.when(s == pl.num_programs(1) - 1)
    def _():
        update(inew_ref[0], lambda g: (kn_ref[0, :, g * HEAD_DIM:(g + 1) * HEAD_DIM],
                                       vn_ref[0, :, g * HEAD_DIM:(g + 1) * HEAD_DIM]))
        o_ref[0] = (acc_ref[...] / l_ref[...]).astype(o_ref.dtype)


def sample_attention(page_table, q, scores, scores_new, thr, k_new, v_new, cache_k_rows, cache_v_rows):
    bs, n_pages = page_table.shape
    kvw = N_KV_HEADS * HEAD_DIM
    step_rows = PPA * PAGE * N_KV_HEADS
    fixed = lambda shape: pl.BlockSpec(shape, lambda b, s, pt: (b, 0, 0))
    grid_spec = pltpu.PrefetchScalarGridSpec(
        num_scalar_prefetch=1,
        grid=(bs, n_pages // PPA),
        in_specs=[fixed((1, N_HEADS * QPAD, HEAD_DIM)),
                  pl.BlockSpec((1, QPAD, PPA * PAGE), lambda b, s, pt: (b, 0, s)),
                  fixed((1, QPAD, PAGE)), fixed((1, QPAD, LANES)),
                  fixed((1, PAGE, kvw)), fixed((1, PAGE, kvw)),
                  pl.BlockSpec(memory_space=pl.ANY), pl.BlockSpec(memory_space=pl.ANY)],
        out_specs=fixed((1, N_HEADS * QPAD, HEAD_DIM)),
        scratch_shapes=[pltpu.VMEM((N_HEADS * QPAD, LANES), F32),
                        pltpu.VMEM((N_HEADS * QPAD, LANES), F32),
                        pltpu.VMEM((N_HEADS * QPAD, HEAD_DIM), F32),
                        pltpu.VMEM((2, step_rows, HEAD_DIM), F32),
                        pltpu.VMEM((2, step_rows, HEAD_DIM), F32),
                        pltpu.SemaphoreType.DMA((2, 2))],
    )
    return pl.pallas_call(
        _sample_attn_kernel,
        grid_spec=grid_spec,
        out_shape=jax.ShapeDtypeStruct((bs, N_HEADS * QPAD, HEAD_DIM), BF16),
        compiler_params=_params("arbitrary", "arbitrary"),
        name="sample_attn",
    )(page_table.reshape(-1), q, scores, scores_new, thr, k_new, v_new, cache_k_rows, cache_v_rows)


def _glu(a, gate):
    return a * jax.nn.sigmoid(gate)


def _token_tile(n_tok, pref):
    return pref if n_tok % pref == 0 else n_tok


def _layer_front(x2, pos_tabs, n_pos_tiles, wts, tm):
    tf = min(tm, 256)
    prep = attn_front(x2, wts["norm_attn"], wts["w_a"], pos_tabs, n_pos_tiles * (tm // tf), tf)
    g = rms_mm(x2, wts["norm_attn"], [wts["w_u1"], wts["w_u2"]], _glu, F32, tm, wts["w_u1"].shape[1])
    return prep, g


def _layer_back(x2, a, c, wts, tm):
    h, n2 = out_proj(a, c, wts["w_oa"], wts["w_oc"], x2, wts["norm_ffn"], min(tm, 256))
    act = ffn_up(n2, wts["w_gate"], wts["w_up"], _token_tile(n2.shape[0], 2 * tm), 512)
    return ffn_down(act, wts["w_down"], h, wts["norm_final"], tm, 512)


def _prep_weights(l, norm_attn, w_in, conv_dw, conv_dw_bias, conv_ln_g, conv_ln_b, w_out, norm_ffn,
                  w_gate, w_up, w_down, norm_final):
    w = w_in[l]
    aw = N_HEADS * HEAD_DIM
    kvw = N_KV_HEADS * HEAD_DIM
    o_small = aw + 2 * kvw + IDX_HEADS * IDX_DIM
    o_u = o_small + IDX_DIM + IDX_HEADS
    cw = (w.shape[1] - o_u) // 2
    pad = jnp.zeros((w.shape[0], LANES - IDX_DIM - IDX_HEADS), w.dtype)
    w_a = jnp.concatenate([w[:, :o_u], pad], axis=1).astype(BF16)
    dw = jnp.concatenate([conv_dw[l], jnp.zeros((HALO - CONV_KERNEL, cw), F32)], axis=0)
    return {
        "norm_attn": norm_attn[l][None, :], "w_a": w_a,
        "w_u1": w[:, o_u:o_u + cw].astype(BF16), "w_u2": w[:, o_u + cw:].astype(BF16),
        "dw": dw, "dw_bias": conv_dw_bias[l][None, :], "ln_g": conv_ln_g[l][None, :], "ln_b": conv_ln_b[l][None, :],
        "w_oa": w_out[l][:aw].astype(BF16), "w_oc": w_out[l][aw:].astype(BF16),
        "norm_ffn": norm_ffn[l][None, :],
        "w_gate": w_gate[l].astype(BF16), "w_up": w_up[l].astype(BF16), "w_down": w_down[l].astype(BF16),
        "norm_final": norm_final[None, :],
    }


def kernel(x_prompt, x_sample, cache_k, cache_v, cache_idx_k, state_conv, page_table, norm_attn, w_in, conv_dw, conv_dw_bias, conv_ln_g, conv_ln_b, w_out, norm_ffn, w_gate, w_up, w_down, norm_final):
    batch, seq, d_model = x_prompt.shape
    bs, ts = x_sample.shape[:2]
    past = page_table.shape[1] * PAGE
    cw = conv_dw.shape[2]
    assert norm_attn.shape[0] == 1, "single-layer step"
    wts = _prep_weights(0, norm_attn, w_in, conv_dw, conv_dw_bias, conv_ln_g, conv_ln_b, w_out, norm_ffn,
                        w_gate, w_up, w_down, norm_final)

    tm = _token_tile(batch * seq, 512)
    xp = x_prompt.reshape(batch * seq, d_model)
    tabs_p = rope_tables(jnp.arange(seq, dtype=jnp.int32))
    (q, kf, kb, vf, vb, qi, kif, kib, wi), g = _layer_front(xp, tabs_p, seq // tm, wts, tm)
    wit = wi.reshape(batch, seq, IDX_HEADS).transpose(0, 2, 1)
    a = prompt_attention(q, kb, vb, qi, kib, wit, batch, seq)
    g3 = g.reshape(batch, seq, cw)
    c = conformer_conv(g3, jnp.zeros((batch, HALO, cw), F32), wts["dw"], wts["dw_bias"], wts["ln_g"], wts["ln_b"],
                       min(seq, 256))
    y_prompt = _layer_back(xp, a, c.reshape(batch * seq, cw), wts, tm).reshape(batch, seq, d_model)
    k_prompt = kf.reshape(1, batch, seq, N_KV_HEADS, HEAD_DIM)
    v_prompt = vf.reshape(1, batch, seq, N_KV_HEADS, HEAD_DIM)
    idx_k_prompt = kif.reshape(1, batch, seq, IDX_DIM)
    conv_prompt = g3[:, seq - (CONV_KERNEL - 1):][None]

    ns = bs * ts
    xs = x_sample.reshape(ns, d_model)
    tabs_s = tuple(jnp.tile(t, (bs, 1)) for t in rope_tables(past + jnp.arange(ts, dtype=jnp.int32)))
    (q, kf, kb, vf, vb, qi, kif, kib, wi), g = _layer_front(xs, tabs_s, 1, wts, ns)

    def pad_q(t):
        return jnp.pad(t, ((0, 0), (0, QPAD - ts)) + ((0, 0),) * (t.ndim - 2))

    def pad_keys(t):
        return jnp.pad(t, ((0, 0), (0, PAGE - ts), (0, 0)))

    qi_s = pad_q(qi.reshape(IDX_HEADS, bs, ts, IDX_DIM).transpose(1, 2, 0, 3))
    qi_s = qi_s.transpose(0, 2, 1, 3).reshape(bs, IDX_HEADS * QPAD, IDX_DIM)
    w_s = pad_q(wi.reshape(bs, ts, IDX_HEADS)).transpose(0, 2, 1).reshape(bs, IDX_HEADS * QPAD, 1)
    ki_new_t = jnp.swapaxes(pad_keys(kib.reshape(bs, ts, IDX_DIM)), 1, 2)
    sc_past, sc_new = sample_scores(page_table, qi_s, w_s, ki_new_t, jnp.swapaxes(cache_idx_k[0], 1, 2))
    sc_all = jnp.concatenate([sc_past[:, :ts], sc_new[:, :ts]], axis=-1).reshape(ns, past + PAGE)
    thr = sample_threshold(sc_all, min(TOPK_MAX, (past + ts) // 4))
    thr = pad_q(thr.reshape(bs, ts, LANES))
    q_s = pad_q(q.reshape(bs, ts, N_HEADS, HEAD_DIM)).transpose(0, 2, 1, 3).reshape(bs, N_HEADS * QPAD, HEAD_DIM)
    kvw = N_KV_HEADS * HEAD_DIM
    o = sample_attention(page_table, q_s, sc_past, sc_new, thr,
                         pad_keys(kb.reshape(bs, ts, kvw)), pad_keys(vb.reshape(bs, ts, kvw)),
                         cache_k[0].reshape(-1, HEAD_DIM), cache_v[0].reshape(-1, HEAD_DIM))
    a = o.reshape(bs, N_HEADS, QPAD, HEAD_DIM)[:, :, :ts].transpose(0, 2, 1, 3).reshape(ns, N_HEADS * HEAD_DIM)

    g3 = g.reshape(bs, ts, cw)
    prev = state_conv[0]
    prev_pad = jnp.pad(prev, ((0, 0), (HALO - (CONV_KERNEL - 1), 0), (0, 0)))
    c = conformer_conv(pad_q(g3), prev_pad, wts["dw"], wts["dw_bias"], wts["ln_g"], wts["ln_b"], QPAD)
    y_sample = _layer_back(xs, a, c[:, :ts].reshape(ns, cw), wts, ns).reshape(bs, ts, d_model)
    k_sample = kf.reshape(1, bs, ts, N_KV_HEADS, HEAD_DIM)
    v_sample = vf.reshape(1, bs, ts, N_KV_HEADS, HEAD_DIM)
    idx_k_sample = kif.reshape(1, bs, ts, IDX_DIM)
    conv_sample = jnp.concatenate([prev, g3], axis=1)[:, ts:][None]

    return (y_prompt, y_sample, k_prompt, v_prompt, idx_k_prompt, conv_prompt,
            k_sample, v_sample, idx_k_sample, conv_sample)
```
